```python
import jax, jax.numpy as jnp
from jax import lax
import numpy as np

D_MODEL = 2048
BATCH = 4
SEQ = 2048
DEPTH = 1
DEC_BATCH = 128
DEC_SEQ = 4
PAST_LEN = 16384
PAGE_SIZE = 128

N_RET_HEADS = 8
RET_DK = D_MODEL // N_RET_HEADS
RET_DV = 2 * RET_DK
D_RET_QK = N_RET_HEADS * RET_DK
D_RET_V = N_RET_HEADS * RET_DV
RET_CHUNK = 128
ROPE_BASE = 10000.0
D_CONV = D_MODEL
CONV_W = 3
D_FF = 5632
FFN_CONV_W = 3
EPS = 1e-6
D_IN = 2 * D_RET_QK + 2 * D_RET_V + 3 * D_CONV + 2 * D_MODEL

kernel_name = "retnet_shortconv_convffn_hybrid_step"


def rmsnorm(x, g):
    xf = x.astype(jnp.float32)
    y = xf * lax.rsqrt(jnp.mean(xf * xf, axis=-1, keepdims=True) + EPS)
    return (y * g.astype(jnp.float32)).astype(x.dtype)


def rotary(x, pos):
    half = x.shape[-1] // 2
    inv = ROPE_BASE ** (-jnp.arange(half, dtype=jnp.float32) / half)
    ang = pos.astype(jnp.float32)[:, None] * inv[None, :]
    cos = jnp.cos(ang)[None, :, None, :]
    sin = jnp.sin(ang)[None, :, None, :]
    xf = x.astype(jnp.float32)
    x1, x2 = xf[..., :half], xf[..., half:]
    return jnp.concatenate([x1 * cos - x2 * sin, x2 * cos + x1 * sin], axis=-1)


def causal_dwconv(u, buf, w):
    T = u.shape[1]
    width = w.shape[0]
    up = jnp.concatenate([buf.astype(u.dtype), u], axis=1)
    out = sum(w[i] * up[:, i:i + T] for i in range(width))
    return out, up[:, T:]


def retention_chunk(S, q, k, v, log_g):
    C = q.shape[2]
    i = jnp.arange(C, dtype=jnp.float32)
    diff = i[:, None] - i[None, :]
    causal = diff >= 0
    decay = jnp.where(causal[None], jnp.exp(log_g[:, None, None] * jnp.where(causal, diff, 0.0)[None]), 0.0)
    scores = jnp.einsum('bhid,bhjd->bhij', q, k) * decay[None]
    inner = jnp.einsum('bhij,bhje->bhie', scores, v)
    cross = jnp.einsum('bhid,bhde->bhie', q, S) * jnp.exp(log_g[:, None] * (i + 1.0))[None, :, :, None]
    k_dec = k * jnp.exp(log_g[:, None] * (C - 1.0 - i))[None, :, :, None]
    S_new = jnp.exp(log_g * C)[None, :, None, None] * S + jnp.einsum('bhjd,bhje->bhde', k_dec, v)
    return S_new, inner + cross


def retention(q, k, v, s0, chunk, log_g):
    B, H, T, _ = q.shape
    n = T // chunk

    def blocks(a):
        return jnp.moveaxis(a.reshape(B, H, n, chunk, a.shape[-1]), 2, 0)

    def step(S, qkv):
        return retention_chunk(S, qkv[0], qkv[1], qkv[2], log_g)

    s_new, o = lax.scan(step, s0, (blocks(q), blocks(k), blocks(v)))
    o = jnp.moveaxis(o, 0, 2).reshape(B, H, T, v.shape[-1])
    return o, s_new


def hybrid_layer(x, pos, s_ret, s_conv, s_ffn, chunk,
                 g_pre_mix, w_in, conv_w, p_ret, p_conv, w_o, g_post_mix,
                 g_pre_ffn, w_up, w_gate, ffn_conv_w, ffn_conv_b, w_down, g_post_ffn):
    B, T, _ = x.shape
    h = rmsnorm(x, g_pre_mix)
    z = h @ w_in
    cuts = np.cumsum([D_RET_QK, D_RET_QK, D_RET_V, D_RET_V, D_CONV, D_CONV, D_CONV, D_MODEL]).tolist()
    q, k, v, g_ret, gb, gc, hc, gate_a, gate_b = jnp.split(z, cuts, axis=-1)

    log_g = jnp.log1p(-jnp.exp2(-5.0 - jnp.arange(N_RET_HEADS, dtype=jnp.float32)))
    qr = rotary(q.reshape(B, T, N_RET_HEADS, RET_DK), pos)
    kr = rotary(k.reshape(B, T, N_RET_HEADS, RET_DK), pos) * (RET_DK ** -0.5)
    vh = v.reshape(B, T, N_RET_HEADS, RET_DV).astype(jnp.float32)
    o, s_ret_new = retention(qr.transpose(0, 2, 1, 3), kr.transpose(0, 2, 1, 3), vh.transpose(0, 2, 1, 3),
                             s_ret.astype(jnp.float32), chunk, log_g)
    o = o * lax.rsqrt(jnp.mean(o * o, axis=-1, keepdims=True) + EPS)
    o = o.transpose(0, 2, 1, 3).reshape(B, T, D_RET_V).astype(x.dtype)
    y_a = (jax.nn.silu(g_ret) * o) @ p_ret

    conv_out, s_conv_new = causal_dwconv(gc * hc, s_conv, conv_w)
    y_b = (gb * conv_out) @ p_conv

    m = jax.nn.sigmoid(gate_a) * y_a + jax.nn.sigmoid(gate_b) * y_b
    x = x + rmsnorm(m @ w_o, g_post_mix)

    h2 = rmsnorm(x, g_pre_ffn)
    u, s_ffn_new = causal_dwconv(h2 @ w_up, s_ffn, ffn_conv_w)
    f = (jax.nn.gelu(u + ffn_conv_b, approximate=True) * (h2 @ w_gate)) @ w_down
    x = x + rmsnorm(f, g_post_ffn)
    return x, s_ret_new.astype(s_ret.dtype), s_conv_new.astype(s_conv.dtype), s_ffn_new.astype(s_ffn.dtype)


def setup_inputs(seed: int = 0) -> dict:
    key = jax.random.key(seed)
    ks = jax.random.split(key, 20)
    f32 = jnp.float32

    def nrm(k, shape, scale):
        return jax.random.normal(k, shape, f32) * scale

    def gain(k, n):
        return 1.0 + 0.1 * jax.random.normal(k, (DEPTH, n), f32)

    return {
        "x_prompt": nrm(ks[0], (BATCH, SEQ, D_MODEL), 1.0),
        "x_sample": nrm(ks[1], (DEC_BATCH, DEC_SEQ, D_MODEL), 1.0),
        "state_ret": nrm(ks[2], (DEPTH, DEC_BATCH, N_RET_HEADS, RET_DK, RET_DV), 0.05),
        "state_conv": nrm(ks[3], (DEPTH, DEC_BATCH, CONV_W - 1, D_CONV), 1.0),
        "state_ffn": nrm(ks[4], (DEPTH, DEC_BATCH, FFN_CONV_W - 1, D_FF), 1.0),
        "g_pre_mix": gain(ks[5], D_MODEL),
        "w_in": nrm(ks[6], (DEPTH, D_MODEL, D_IN), D_MODEL ** -0.5),
        "conv_w": nrm(ks[7], (DEPTH, CONV_W, D_CONV), CONV_W ** -0.5),
        "p_ret": nrm(ks[8], (DEPTH, D_RET_V, D_MODEL), D_RET_V ** -0.5),
        "p_conv": nrm(ks[9], (DEPTH, D_CONV, D_MODEL), D_CONV ** -0.5),
        "w_o": nrm(ks[10], (DEPTH, D_MODEL, D_MODEL), D_MODEL ** -0.5),
        "g_post_mix": gain(ks[11], D_MODEL),
        "g_pre_ffn": gain(ks[12], D_MODEL),
        "w_up": nrm(ks[13], (DEPTH, D_MODEL, D_FF), D_MODEL ** -0.5),
        "w_gate": nrm(ks[14], (DEPTH, D_MODEL, D_FF), D_MODEL ** -0.5),
        "ffn_conv_w": nrm(ks[15], (DEPTH, FFN_CONV_W, D_FF), FFN_CONV_W ** -0.5),
        "ffn_conv_b": nrm(ks[16], (DEPTH, D_FF), 0.02),
        "w_down": nrm(ks[17], (DEPTH, D_FF, D_MODEL), D_FF ** -0.5),
        "g_post_ffn": gain(ks[18], D_MODEL),
    }


def reference(x_prompt, x_sample, state_ret, state_conv, state_ffn,
              g_pre_mix, w_in, conv_w, p_ret, p_conv, w_o, g_post_mix,
              g_pre_ffn, w_up, w_gate, ffn_conv_w, ffn_conv_b, w_down, g_post_ffn):
    pos_p = jnp.arange(SEQ, dtype=jnp.int32)
    pos_s = PAST_LEN + jnp.arange(DEC_SEQ, dtype=jnp.int32)
    yp, ys = x_prompt, x_sample
    rp, cp, fp, rs, cs, fs = [], [], [], [], [], []
    for l in range(DEPTH):
        w = (g_pre_mix[l], w_in[l], conv_w[l], p_ret[l], p_conv[l], w_o[l], g_post_mix[l],
             g_pre_ffn[l], w_up[l], w_gate[l], ffn_conv_w[l], ffn_conv_b[l], w_down[l], g_post_ffn[l])
        z_ret = jnp.zeros((BATCH, N_RET_HEADS, RET_DK, RET_DV), state_ret.dtype)
        z_conv = jnp.zeros((BATCH, CONV_W - 1, D_CONV), state_conv.dtype)
        z_ffn = jnp.zeros((BATCH, FFN_CONV_W - 1, D_FF), state_ffn.dtype)
        yp, a, b, c = hybrid_layer(yp, pos_p, z_ret, z_conv, z_ffn, RET_CHUNK, *w)
        ys, d, e, f = hybrid_layer(ys, pos_s, state_ret[l], state_conv[l], state_ffn[l], DEC_SEQ, *w)
        rp.append(a); cp.append(b); fp.append(c)
        rs.append(d); cs.append(e); fs.append(f)
    return (yp, ys, jnp.stack(rp), jnp.stack(cp), jnp.stack(fp), jnp.stack(rs), jnp.stack(cs), jnp.stack(fs))
```

```python
import functools

import numpy as np
import jax
import jax.numpy as jnp
from jax import lax
from jax.experimental import pallas as pl
from jax.experimental.pallas import tpu as pltpu

D_MODEL = 2048
BATCH = 4
SEQ = 2048
DEC_BATCH = 128
DEC_SEQ = 4
PAST_LEN = 16384
N_HEADS = 8
DK = D_MODEL // N_HEADS
DV = 2 * DK
D_QK = N_HEADS * DK
D_V = N_HEADS * DV
CHUNK = 128
ROPE_BASE = 10000.0
D_CONV = D_MODEL
D_FF = 5632
EPS = 1e-6
D_IN = 2 * D_QK + 2 * D_V + 3 * D_CONV + 2 * D_MODEL

F32 = jnp.float32
BF16 = jnp.bfloat16

TM = 512
TN = 512
TILES_PER_SEQ = SEQ // TM
N_QK_TILES = 2 * D_QK // TN
N_Q_TILES = D_QK // TN
N_V_TILES = D_V // TN
REST_W = D_V + 3 * D_CONV + 2 * D_MODEL
REST_GB = D_V // TN
REST_GC = REST_GB + D_CONV // TN
REST_HC = REST_GC + D_CONV // TN
REST_GA = REST_HC + D_CONV // TN
REST_GBT = REST_GA + D_MODEL // TN
DEC_PAD = 8
DEC_GROUP = 128 // DEC_PAD
VMEM_LIMIT = 56 * 1024 * 1024


def _params(sem):
    return pltpu.CompilerParams(dimension_semantics=sem, vmem_limit_bytes=VMEM_LIMIT)


def _rms(x, g):
    ms = jnp.mean(x * x, axis=-1, keepdims=True)
    return x * lax.rsqrt(ms + EPS) * g


def _log_g():
    return np.log1p(-np.exp2(-5.0 - np.arange(N_HEADS, dtype=np.float64)))


def _rope_tables():
    half = DK // 2
    inv = ROPE_BASE ** (-np.arange(half, dtype=np.float64) / half)
    pos_p = np.arange(SEQ, dtype=np.float64)
    pos_s = np.repeat(PAST_LEN + np.arange(DEC_SEQ, dtype=np.float64), DEC_BATCH)
    out = []
    for pos in (pos_p, pos_s):
        ang = pos[:, None] * inv[None, :]
        out.append((jnp.asarray(np.cos(ang), F32), jnp.asarray(np.sin(ang), F32)))
    return out


def _ret_tables_prompt():
    lg = _log_g()[:, None, None]
    i = np.arange(CHUNK, dtype=np.float64)
    diff = i[:, None] - i[None, :]
    decay = np.where(diff >= 0, np.exp(lg * np.where(diff >= 0, diff, 0.0)), 0.0)
    ones = np.ones((1, 1, 128))
    arow = np.exp(lg * (i + 1.0)[None, :, None]) * ones
    crow = np.exp(lg * (CHUNK - 1.0 - i)[None, :, None]) * ones
    gC = np.exp(_log_g() * CHUNK)
    return (jnp.asarray(decay, F32), jnp.asarray(arow, F32), jnp.asarray(crow, F32), jnp.asarray(gC, F32))


def _ret_tables_decode():
    lg = _log_g()[:, None, None]
    r = np.arange(128)
    seq, t = r // DEC_PAD, (r % DEC_PAD).astype(np.float64)
    valid = t < DEC_SEQ
    diff = t[:, None] - t[None, :]
    ok = (seq[:, None] == seq[None, :]) & (diff >= 0) & valid[:, None] & valid[None, :]
    decay = np.where(ok[None], np.exp(lg * np.where(ok, diff, 0.0)[None]), 0.0)
    ones = np.ones((1, 1, 128))
    arow = np.where(valid[None, :, None], np.exp(lg * (t + 1.0)[None, :, None]), 0.0) * ones
    crow = np.where(valid[None, :, None], np.exp(lg * (DEC_SEQ - 1.0 - t)[None, :, None]), 0.0) * ones
    gC = np.exp(_log_g() * DEC_SEQ)
    return (jnp.asarray(decay, F32), jnp.asarray(arow, F32), jnp.asarray(crow, F32), jnp.asarray(gC, F32))


def _inproj_kernel(x_ref, g_ref, w_ref, cos_ref, sin_ref, qk_ref, v_ref, rest_ref, h_scr):
    j = pl.program_id(1)

    @pl.when(j == 0)
    def _():
        h_scr[...] = _rms(x_ref[...], g_ref[...]).astype(BF16)

    z = jnp.dot(h_scr[...], w_ref[...], preferred_element_type=F32)

    @pl.when(j < N_QK_TILES)
    def _():
        scale = jnp.where(j < N_Q_TILES, 1.0, DK ** -0.5).astype(F32)
        c = cos_ref[...]
        s = sin_ref[...]
        half = DK // 2
        for hh in range(TN // DK):
            x1 = z[:, hh * DK:hh * DK + half]
            x2 = z[:, hh * DK + half:(hh + 1) * DK]
            qk_ref[:, hh * DK:hh * DK + half] = ((x1 * c - x2 * s) * scale).astype(BF16)
            qk_ref[:, hh * DK + half:(hh + 1) * DK] = ((x2 * c + x1 * s) * scale).astype(BF16)

    @pl.when((j >= N_QK_TILES) & (j < N_QK_TILES + N_V_TILES))
    def _():
        v_ref[...] = z.astype(BF16)

    @pl.when(j >= N_QK_TILES + N_V_TILES)
    def _():
        rest_ref[...] = z


def _inproj(x, g, w, cos, sin):
    rows = x.shape[0]
    ntab = cos.shape[0] // TM
    nv0 = N_QK_TILES
    nr0 = N_QK_TILES + N_V_TILES
    return pl.pallas_call(
        _inproj_kernel,
        grid=(rows // TM, D_IN // TN),
        in_specs=[
            pl.BlockSpec((TM, D_MODEL), lambda i, j: (i, 0)),
            pl.BlockSpec((1, D_MODEL), lambda i, j: (0, 0)),
            pl.BlockSpec((D_MODEL, TN), lambda i, j: (0, j)),
            pl.BlockSpec((TM, DK // 2), lambda i, j: (i % ntab, 0)),
            pl.BlockSpec((TM, DK // 2), lambda i, j: (i % ntab, 0)),
        ],
        out_specs=[
            pl.BlockSpec((TM, TN), lambda i, j: (i, jnp.minimum(j, nv0 - 1))),
            pl.BlockSpec((TM, TN), lambda i, j: (i, jnp.clip(j - nv0, 0, N_V_TILES - 1))),
            pl.BlockSpec((TM, TN), lambda i, j: (i, jnp.clip(j - nr0, 0, REST_W // TN - 1))),
        ],
        out_shape=[
            jax.ShapeDtypeStruct((rows, 2 * D_QK), BF16),
            jax.ShapeDtypeStruct((rows, D_V), BF16),
            jax.ShapeDtypeStruct((rows, REST_W), F32),
        ],
        scratch_shapes=[pltpu.VMEM((TM, D_MODEL), BF16)],
        compiler_params=_params(("arbitrary", "arbitrary")),
        name="inproj",
    )(x, g, w, cos, sin)


def _shift_rows(u, c0, c1):
    rows = lax.broadcasted_iota(jnp.int32, u.shape, 0)
    r1 = pltpu.roll(u, 1, 0)
    r2 = pltpu.roll(u, 2, 0)
    u1 = jnp.where(rows == 0, c1, r1)
    u2 = jnp.where(rows == 0, c0, jnp.where(rows == 1, c1, r2))
    return u1, u2


def _shift_slabs(u, st):
    n = DEC_BATCH
    u1 = jnp.concatenate([st[n:2 * n], u[0:(DEC_SEQ - 1) * n]], axis=0)
    u2 = jnp.concatenate([st, u[0:(DEC_SEQ - 2) * n]], axis=0)
    return u1, u2


def _conv_p_kernel(gb_ref, gc_ref, hc_ref, w_ref, cb_ref, st_ref, carry):
    i = pl.program_id(1)

    @pl.when(i % TILES_PER_SEQ == 0)
    def _():
        carry[...] = jnp.zeros_like(carry)

    u = gc_ref[...] * hc_ref[...]
    u1, u2 = _shift_rows(u, carry[0:1, :], carry[1:2, :])
    w = w_ref[...]
    conv = w[0:1] * u2 + w[1:2] * u1 + w[2:3] * u
    cb_ref[...] = (gb_ref[...] * conv).astype(BF16)
    last2 = u[TM - 2:TM, :]
    carry[0:2, :] = last2

    @pl.when(i % TILES_PER_SEQ == TILES_PER_SEQ - 1)
    def _():
        st_ref[0] = last2


def _conv_prompt(rest, conv_w):
    rows = rest.shape[0]
    ncol = D_CONV // TN
    return pl.pallas_call(
        _conv_p_kernel,
        grid=(ncol, rows // TM),
        in_specs=[
            pl.BlockSpec((TM, TN), lambda jc, i: (i, REST_GB + jc)),
            pl.BlockSpec((TM, TN), lambda jc, i: (i, REST_GC + jc)),
            pl.BlockSpec((TM, TN), lambda jc, i: (i, REST_HC + jc)),
            pl.BlockSpec((3, TN), lambda jc, i: (0, jc)),
        ],
        out_specs=[
            pl.BlockSpec((TM, TN), lambda jc, i: (i, jc)),
            pl.BlockSpec((1, 2, TN), lambda jc, i: (i // TILES_PER_SEQ, 0, jc)),
        ],
        out_shape=[
            jax.ShapeDtypeStruct((rows, D_CONV), BF16),
            jax.ShapeDtypeStruct((rows // SEQ, 2, D_CONV), F32),
        ],
        scratch_shapes=[pltpu.VMEM((8, TN), F32)],
        compiler_params=_params(("arbitrary", "arbitrary")),
        name="conv_prompt",
    )(rest, rest, rest, conv_w)


def _conv_d_kernel(gb_ref, gc_ref, hc_ref, w_ref, st_ref, cb_ref, stn_ref):
    u = gc_ref[...] * hc_ref[...]
    u1, u2 = _shift_slabs(u, st_ref[...])
    w = w_ref[...]
    conv = w[0:1] * u2 + w[1:2] * u1 + w[2:3] * u
    cb_ref[...] = (gb_ref[...] * conv).astype(BF16)
    stn_ref[...] = u[(DEC_SEQ - 2) * DEC_BATCH:, :]


def _conv_decode(rest, conv_w, st):
    rows = rest.shape[0]
    ncol = D_CONV // TN
    return pl.pallas_call(
        _conv_d_kernel,
        grid=(ncol,),
        in_specs=[
            pl.BlockSpec((rows, TN), lambda jc: (0, REST_GB + jc)),
            pl.BlockSpec((rows, TN), lambda jc: (0, REST_GC + jc)),
            pl.BlockSpec((rows, TN), lambda jc: (0, REST_HC + jc)),
            pl.BlockSpec((3, TN), lambda jc: (0, jc)),
            pl.BlockSpec((2 * DEC_BATCH, TN), lambda jc: (0, jc)),
        ],
        out_specs=[
            pl.BlockSpec((rows, TN), lambda jc: (0, jc)),
            pl.BlockSpec((2 * DEC_BATCH, TN), lambda jc: (0, jc)),
        ],
        out_shape=[
            jax.ShapeDtypeStruct((rows, D_CONV), BF16),
            jax.ShapeDtypeStruct((2 * DEC_BATCH, D_CONV), F32),
        ],
        compiler_params=_params(("arbitrary",)),
        name="conv_decode",
    )(rest, rest, rest, conv_w, st)


def _tile_lanes(a, n):
    return jnp.concatenate([a] * n, axis=1)


def _gate_norm(o, g):
    on = o * lax.rsqrt(jnp.mean(o * o, axis=-1, keepdims=True) + EPS)
    return (jax.nn.silu(g) * on).astype(BF16)


def _ret_p_kernel(gc_ref, q_ref, k_ref, v_ref, g_ref, dec_ref, ar_ref, cr_ref, og_ref, st_ref, s_scr):
    h = pl.program_id(1)
    gC = gc_ref[h]
    decay = dec_ref[0]
    arow = _tile_lanes(ar_ref[0], DV // 128)
    crow = _tile_lanes(cr_ref[0], DK // 128)
    s_scr[...] = jnp.zeros_like(s_scr)

    def chunk(c, carry):
        r = pl.multiple_of(c * CHUNK, CHUNK)
        q = q_ref[pl.ds(r, CHUNK), :]
        k = k_ref[pl.ds(r, CHUNK), :]
        v = v_ref[pl.ds(r, CHUNK), :]
        s_prev = s_scr[...]
        sc = lax.dot_general(q, k, (((1,), (1,)), ((), ())), preferred_element_type=F32) * decay
        inner = jnp.dot(sc.astype(BF16), v, preferred_element_type=F32)
        cross = jnp.dot(q, s_prev.astype(BF16), preferred_element_type=F32) * arow
        kdt = (k.astype(F32) * crow).T.astype(BF16)
        s_scr[...] = gC * s_prev + jnp.dot(kdt, v, preferred_element_type=F32)
        og_ref[pl.ds(r, CHUNK), :] = _gate_norm(inner + cross, g_ref[pl.ds(r, CHUNK), :])
        return carry

    lax.fori_loop(0, SEQ // CHUNK, chunk, 0)
    st_ref[0, 0] = s_scr[...]


def _ret_prompt(qk, v, rest, tables):
    decay, arow, crow, gC = tables
    nb = qk.shape[0] // SEQ
    tab = pl.BlockSpec((1, CHUNK, 128), lambda b, h: (h, 0, 0))
    return pl.pallas_call(
        _ret_p_kernel,
        grid=(nb, N_HEADS),
        in_specs=[
            pl.BlockSpec(memory_space=pltpu.SMEM),
            pl.BlockSpec((SEQ, DK), lambda b, h: (b, h)),
            pl.BlockSpec((SEQ, DK), lambda b, h: (b, N_HEADS + h)),
            pl.BlockSpec((SEQ, DV), lambda b, h: (b, h)),
            pl.BlockSpec((SEQ, DV), lambda b, h: (b, h)),
            tab, tab, tab,
        ],
        out_specs=[
            pl.BlockSpec((SEQ, DV), lambda b, h: (b, h)),
            pl.BlockSpec((1, 1, DK, DV), lambda b, h: (b, h, 0, 0)),
        ],
        out_shape=[
            jax.ShapeDtypeStruct((nb * SEQ, D_V), BF16),
            jax.ShapeDtypeStruct((nb, N_HEADS, DK, DV), F32),
        ],
        scratch_shapes=[pltpu.VMEM((DK, DV), F32)],
        compiler_params=_params(("arbitrary", "arbitrary")),
        name="ret_prompt",
    )(gC, qk, qk, v, rest, decay, arow, crow)


def _ret_d_kernel(gc_ref, q_ref, k_ref, v_ref, g_ref, s_ref, dec_ref, ar_ref, cr_ref,
                  og_ref, sn_ref, cross_scr, kdt_scr):
    h = pl.program_id(1)
    gC = gc_ref[h]
    q = q_ref[...]
    k = k_ref[...]
    v = v_ref[...]
    vb = v.astype(BF16)
    sc = lax.dot_general(q.astype(BF16), k.astype(BF16), (((1,), (1,)), ((), ())),
                         preferred_element_type=F32) * dec_ref[0]
    inner = jnp.dot(sc.astype(BF16), vb, preferred_element_type=F32)
    kdt_scr[...] = (k * _tile_lanes(cr_ref[0], DK // 128)).T.astype(BF16)
    row_seq = lax.shift_right_logical(lax.broadcasted_iota(jnp.int32, v.shape, 0), DEC_PAD.bit_length() - 1)

    def body(bi, carry):
        s_prev = s_ref[0, bi, 0]
        r = pl.multiple_of(bi * DEC_PAD, DEC_PAD)
        qs = q_ref[pl.ds(r, DEC_PAD), :].astype(BF16)
        cross_scr[pl.ds(r, DEC_PAD), :] = jnp.dot(qs, s_prev.astype(BF16), preferred_element_type=F32)
        vm = jnp.where(row_seq == bi, v, 0.0).astype(BF16)
        sn_ref[0, bi, 0] = gC * s_prev + jnp.dot(kdt_scr[...], vm, preferred_element_type=F32)
        return carry

    lax.fori_loop(0, DEC_GROUP, body, 0)
    o = inner + cross_scr[...] * _tile_lanes(ar_ref[0], DV // 128)
    og_ref[...] = _gate_norm(o, g_ref[...])


def _ret_decode(q, k, v, g, state, tables):
    decay, arow, crow, gC = tables
    ngrp = DEC_BATCH // DEC_GROUP
    tab = pl.BlockSpec((1, 128, 128), lambda b, h: (h, 0, 0))
    sspec = pl.BlockSpec((1, DEC_GROUP, 1, DK, DV), lambda b, h: (0, b, h, 0, 0))
    return pl.pallas_call(
        _ret_d_kernel,
        grid=(ngrp, N_HEADS),
        in_specs=[
            pl.BlockSpec(memory_space=pltpu.SMEM),
            pl.BlockSpec((128, DK), lambda b, h: (b, h)),
            pl.BlockSpec((128, DK), lambda b, h: (b, h)),
            pl.BlockSpec((128, DV), lambda b, h: (b, h)),
            pl.BlockSpec((128, DV), lambda b, h: (b, h)),
            sspec, tab, tab, tab,
        ],
        out_specs=[pl.BlockSpec((128, DV), lambda b, h: (b, h)), sspec],
        out_shape=[
            jax.ShapeDtypeStruct((DEC_BATCH * DEC_PAD, D_V), BF16),
            jax.ShapeDtypeStruct(state.shape, F32),
        ],
        scratch_shapes=[pltpu.VMEM((128, DV), F32), pltpu.VMEM((DK, 128), BF16)],
        compiler_params=_params(("arbitrary", "arbitrary")),
        name="ret_decode",
    )(gC, q, k, v, g, state, decay, arow, crow)


def _merge_kernel(og_ref, cb_ref, ga_ref, gb_ref, pr_ref, pc_ref, m_ref):
    ya = jnp.dot(og_ref[...], pr_ref[...], preferred_element_type=F32)
    yb = jnp.dot(cb_ref[...], pc_ref[...], preferred_element_type=F32)
    m = jax.nn.sigmoid(ga_ref[...]) * ya + jax.nn.sigmoid(gb_ref[...]) * yb
    m_ref[...] = m.astype(BF16)


def _merge(og, cb, rest, p_ret, p_conv):
    rows = og.shape[0]
    return pl.pallas_call(
        _merge_kernel,
        grid=(rows // TM, D_MODEL // TN),
        in_specs=[
            pl.BlockSpec((TM, D_V), lambda i, j: (i, 0)),
            pl.BlockSpec((TM, D_CONV), lambda i, j: (i, 0)),
            pl.BlockSpec((TM, TN), lambda i, j: (i, REST_GA + j)),
            pl.BlockSpec((TM, TN), lambda i, j: (i, REST_GBT + j)),
            pl.BlockSpec((D_V, TN), lambda i, j: (0, j)),
            pl.BlockSpec((D_CONV, TN), lambda i, j: (0, j)),
        ],
        out_specs=pl.BlockSpec((TM, TN), lambda i, j: (i, j)),
        out_shape=jax.ShapeDtypeStruct((rows, D_MODEL), BF16),
        compiler_params=_params(("arbitrary", "arbitrary")),
        name="merge",
    )(og, cb, rest, rest, p_ret, p_conv)


def _oproj_kernel(m_ref, x_ref, w_ref, g_ref, o_ref):
    mix = jnp.dot(m_ref[...], w_ref[...], preferred_element_type=F32)
    o_ref[...] = x_ref[...] + _rms(mix, g_ref[...])


def _oproj(m, x, w_o, g):
    rows = m.shape[0]
    return pl.pallas_call(
        _oproj_kernel,
        grid=(rows // TM,),
        in_specs=[
            pl.BlockSpec((TM, D_MODEL), lambda i: (i, 0)),
            pl.BlockSpec((TM, D_MODEL), lambda i: (i, 0)),
            pl.BlockSpec((D_MODEL, D_MODEL), lambda i: (0, 0)),
            pl.BlockSpec((1, D_MODEL), lambda i: (0, 0)),
        ],
        out_specs=pl.BlockSpec((TM, D_MODEL), lambda i: (i, 0)),
        out_shape=jax.ShapeDtypeStruct((rows, D_MODEL), F32),
        compiler_params=_params(("arbitrary",)),
        name="oproj",
    )(m, x, w_o, g)


def _ffn_body(decode, x_ref, gpre_ref, wu_ref, wg_ref, cw_ref, cb_ref, wd_ref, gpost_ref, st_ref,
              y_ref, stn_ref, h_scr, acc_scr, carry):
    i = pl.program_id(0)
    j = pl.program_id(1)
    nj = pl.num_programs(1)

    @pl.when(j == 0)
    def _():
        h_scr[...] = _rms(x_ref[...], gpre_ref[...]).astype(BF16)
        acc_scr[...] = jnp.zeros_like(acc_scr)

    h = h_scr[...]
    up = jnp.dot(h, wu_ref[...], preferred_element_type=F32)
    gate = jnp.dot(h, wg_ref[...], preferred_element_type=F32)
    cols = pl.ds(pl.multiple_of(j * TN, TN), TN)
    if decode:
        u1, u2 = _shift_slabs(up, st_ref[...])
        stn_ref[...] = up[(DEC_SEQ - 2) * DEC_BATCH:, :]
    else:
        @pl.when(i % TILES_PER_SEQ == 0)
        def _():
            carry[:, cols] = jnp.zeros((8, TN), F32)

        u1, u2 = _shift_rows(up, carry[0:1, cols], carry[1:2, cols])
        last2 = up[TM - 2:TM, :]
        carry[0:2, cols] = last2

        @pl.when(i % TILES_PER_SEQ == TILES_PER_SEQ - 1)
        def _():
            stn_ref[i // TILES_PER_SEQ, :, cols] = last2

    w = cw_ref[...]
    u = w[0:1] * u2 + w[1:2] * u1 + w[2:3] * up
    act = jax.nn.gelu(u + cb_ref[...], approximate=True) * gate
    acc_scr[...] += jnp.dot(act.astype(BF16), wd_ref[...], preferred_element_type=F32)

    @pl.when(j == nj - 1)
    def _():
        y_ref[...] = x_ref[...] + _rms(acc_scr[...], gpost_ref[...])


def _ffn(x, g_pre, w_up, w_gate, conv_w, conv_b, w_down, g_post, st, decode):
    rows = x.shape[0]
    nseq = rows // SEQ
    if decode:
        st_in = pl.BlockSpec((2 * DEC_BATCH, TN), lambda i, j: (0, j))
        st_out = pl.BlockSpec((2 * DEC_BATCH, TN), lambda i, j: (0, j))
        st_shape = jax.ShapeDtypeStruct((2 * DEC_BATCH, D_FF), F32)
    else:
        st_in = pl.BlockSpec((8, 128), lambda i, j: (0, 0))
        st_out = pl.BlockSpec((nseq, 2, D_FF), lambda i, j: (0, 0, 0))
        st_shape = jax.ShapeDtypeStruct((nseq, 2, D_FF), F32)
    return pl.pallas_call(
        functools.partial(_ffn_body, decode),
        grid=(rows // TM, D_FF // TN),
        in_specs=[
            pl.BlockSpec((TM, D_MODEL), lambda i, j: (i, 0)),
            pl.BlockSpec((1, D_MODEL), lambda i, j: (0, 0)),
            pl.BlockSpec((D_MODEL, TN), lambda i, j: (0, j)),
            pl.BlockSpec((D_MODEL, TN), lambda i, j: (0, j)),
            pl.BlockSpec((3, TN), lambda i, j: (0, j)),
            pl.BlockSpec((1, TN), lambda i, j: (0, j)),
            pl.BlockSpec((TN, D_MODEL), lambda i, j: (j, 0)),
            pl.BlockSpec((1, D_MODEL), lambda i, j: (0, 0)),
            st_in,
        ],
        out_specs=[pl.BlockSpec((TM, D_MODEL), lambda i, j: (i, 0)), st_out],
        out_shape=[jax.ShapeDtypeStruct((rows, D_MODEL), F32), st_shape],
        scratch_shapes=[
            pltpu.VMEM((TM, D_MODEL), BF16),
            pltpu.VMEM((TM, D_MODEL), F32),
            pltpu.VMEM((8, D_FF), F32),
        ],
        compiler_params=_params(("arbitrary", "arbitrary")),
        name="ffn_decode" if decode else "ffn_prompt",
    )(x, g_pre, w_up, w_gate, conv_w, conv_b, w_down, g_post, st)


def _to_time_major(a):
    return a.transpose(1, 0, 2).reshape(a.shape[1] * a.shape[0], a.shape[2])


def _from_time_major(a, t):
    return a.reshape(t, DEC_BATCH, a.shape[1]).transpose(1, 0, 2)


def _to_padded_batch_major(a):
    a = _from_time_major(a, DEC_SEQ).astype(F32)
    a = jnp.pad(a, ((0, 0), (0, DEC_PAD - DEC_SEQ), (0, 0)))
    return a.reshape(DEC_BATCH * DEC_PAD, a.shape[2])


def _layer(xp, xs, s_ret, s_conv, s_ffn, rope, tabs_p, tabs_d,
           g_pre_mix, w_in, conv_w, p_ret, p_conv, w_o, g_post_mix,
           g_pre_ffn, w_up, w_gate, ffn_conv_w, ffn_conv_b, w_down, g_post_ffn):
    (cos_p, sin_p), (cos_s, sin_s) = rope
    g_pre_mix = g_pre_mix.reshape(1, D_MODEL)
    g_post_mix = g_post_mix.reshape(1, D_MODEL)
    g_pre_ffn = g_pre_ffn.reshape(1, D_MODEL)
    g_post_ffn = g_post_ffn.reshape(1, D_MODEL)
    ffn_conv_b = ffn_conv_b.reshape(1, D_FF)
    w_in, p_ret, p_conv, w_o = (a.astype(BF16) for a in (w_in, p_ret, p_conv, w_o))
    w_up, w_gate, w_down = (a.astype(BF16) for a in (w_up, w_gate, w_down))

    qk_p, v_p, rest_p = _inproj(xp, g_pre_mix, w_in, cos_p, sin_p)
    qk_s, v_s, rest_s = _inproj(xs, g_pre_mix, w_in, cos_s, sin_s)

    cb_p, conv_st_p = _conv_prompt(rest_p, conv_w)
    cb_s, conv_st_s = _conv_decode(rest_s, conv_w, _to_time_major(s_conv))

    og_p, ret_st_p = _ret_prompt(qk_p, v_p, rest_p, tabs_p)
    og_s8, ret_st_s = _ret_decode(
        _to_padded_batch_major(qk_s[:, :D_QK]), _to_padded_batch_major(qk_s[:, D_QK:]),
        _to_padded_batch_major(v_s), _to_padded_batch_major(rest_s[:, :D_V]),
        s_ret[None], tabs_d)
    og_s = _to_time_major(og_s8.reshape(DEC_BATCH, DEC_PAD, D_V)[:, :DEC_SEQ])

    m_p = _merge(og_p, cb_p, rest_p, p_ret, p_conv)
    m_s = _merge(og_s, cb_s, rest_s, p_ret, p_conv)
    x1_p = _oproj(m_p, xp, w_o, g_post_mix)
    x1_s = _oproj(m_s, xs, w_o, g_post_mix)

    dummy = jnp.zeros((8, 128), F32)
    y_p, ffn_st_p = _ffn(x1_p, g_pre_ffn, w_up, w_gate, ffn_conv_w, ffn_conv_b, w_down, g_post_ffn,
                         dummy, decode=False)
    y_s, ffn_st_s = _ffn(x1_s, g_pre_ffn, w_up, w_gate, ffn_conv_w, ffn_conv_b, w_down, g_post_ffn,
                         _to_time_major(s_ffn), decode=True)
    return (y_p, y_s, ret_st_p, conv_st_p, ffn_st_p,
            ret_st_s[0], _from_time_major(conv_st_s, 2), _from_time_major(ffn_st_s, 2))


def kernel(x_prompt, x_sample, state_ret, state_conv, state_ffn, g_pre_mix, w_in, conv_w, p_ret, p_conv, w_o,
           g_post_mix, g_pre_ffn, w_up, w_gate, ffn_conv_w, ffn_conv_b, w_down, g_post_ffn):
    rope = _rope_tables()
    tabs_p = _ret_tables_prompt()
    tabs_d = _ret_tables_decode()
    depth = w_in.shape[0]
    xp = x_prompt.reshape(BATCH * SEQ, D_MODEL)
    xs = _to_time_major(x_sample)
    rp, cp, fp, rs, cs, fs = [], [], [], [], [], []
    for l in range(depth):
        xp, xs, a, b, c, d, e, f = _layer(
            xp, xs, state_ret[l], state_conv[l], state_ffn[l], rope, tabs_p, tabs_d,
            g_pre_mix[l], w_in[l], conv_w[l], p_ret[l], p_conv[l], w_o[l], g_post_mix[l],
            g_pre_ffn[l], w_up[l], w_gate[l], ffn_conv_w[l], ffn_conv_b[l], w_down[l], g_post_ffn[l])
        rp.append(a); cp.append(b); fp.append(c)
        rs.append(d); cs.append(e); fs.append(f)
    y_prompt = xp.reshape(BATCH, SEQ, D_MODEL)
    y_sample = _from_time_major(xs, DEC_SEQ)
    return (y_prompt, y_sample, jnp.stack(rp), jnp.stack(cp), jnp.stack(fp),
            jnp.stack(rs), jnp.stack(cs), jnp.stack(fs))
```

```python
import functools

import numpy as np
import jax
import jax.numpy as jnp
from jax import lax
from jax.experimental import pallas as pl
from jax.experimental.pallas import tpu as pltpu

D_MODEL = 2048
BATCH = 4
SEQ = 2048
DEC_BATCH = 128
DEC_SEQ = 4
PAST_LEN = 16384
N_HEADS = 8
DK = D_MODEL // N_HEADS
DV = 2 * DK
D_QK = N_HEADS * DK
D_V = N_HEADS * DV
CHUNK = 128
ROPE_BASE = 10000.0
D_CONV = D_MODEL
D_FF = 5632
EPS = 1e-6
D_IN = 2 * D_QK + 2 * D_V + 3 * D_CONV + 2 * D_MODEL

F32 = jnp.float32
BF16 = jnp.bfloat16

ROWS_P = BATCH * SEQ
ROWS_S = DEC_SEQ * DEC_BATCH
TMP = 1024
NP = ROWS_P // TMP
TPS = SEQ // TMP
SUB = 512
TNW = 1024
TC = 512
TMR = 256
TMO = 512
COL_V = 2 * D_QK
COL_G = COL_V + D_V
COL_GB = COL_G + D_V
COL_GC = COL_GB + D_CONV
COL_HC = COL_GC + D_CONV
COL_GA = COL_HC + D_CONV
GG_W = D_V + 2 * D_MODEL
DEC_PAD = 8
DEC_GROUP = 128 // DEC_PAD
VMEM_LIMIT = 56 * 1024 * 1024


def _params(sem):
    return pltpu.CompilerParams(dimension_semantics=sem, vmem_limit_bytes=VMEM_LIMIT)


def _rms(x, g):
    ms = jnp.mean(x * x, axis=-1, keepdims=True)
    return x * lax.rsqrt(ms + EPS) * g


def _resident(shape):
    return pl.BlockSpec(shape, lambda *_: (0,) * len(shape), pipeline_mode=pl.Buffered(1))


def _log_g():
    return np.log1p(-np.exp2(-5.0 - np.arange(N_HEADS, dtype=np.float64)))


def _rope_tables():
    half = DK // 2
    inv = ROPE_BASE ** (-np.arange(half, dtype=np.float64) / half)
    pos_p = np.arange(SEQ, dtype=np.float64)
    pos_s = np.repeat(PAST_LEN + np.arange(DEC_SEQ, dtype=np.float64), DEC_BATCH)
    out = []
    for pos in (pos_p, pos_s):
        ang = pos[:, None] * inv[None, :]
        out += [jnp.asarray(np.cos(ang), F32), jnp.asarray(np.sin(ang), F32)]
    return out


def _ret_tables_prompt():
    lg = _log_g()[:, None, None]
    i = np.arange(CHUNK, dtype=np.float64)
    diff = i[:, None] - i[None, :]
    decay = np.where(diff >= 0, np.exp(lg * np.where(diff >= 0, diff, 0.0)), 0.0)
    ones = np.ones((1, 1, 128))
    arow = np.exp(lg * (i + 1.0)[None, :, None]) * ones
    crow = np.exp(lg * (CHUNK - 1.0 - i)[None, :, None]) * ones
    gC = np.exp(_log_g() * CHUNK)
    return (jnp.asarray(decay, F32), jnp.asarray(arow, F32), jnp.asarray(crow, F32), jnp.asarray(gC, F32))


def _ret_tables_decode():
    lg = _log_g()[:, None, None]
    r = np.arange(128)
    seq, t = r // DEC_PAD, (r % DEC_PAD).astype(np.float64)
    valid = t < DEC_SEQ
    diff = t[:, None] - t[None, :]
    ok = (seq[:, None] == seq[None, :]) & (diff >= 0) & valid[:, None] & valid[None, :]
    decay = np.where(ok[None], np.exp(lg * np.where(ok, diff, 0.0)[None]), 0.0)
    ones = np.ones((1, 1, 128))
    arow = np.where(valid[None, :, None], np.exp(lg * (t + 1.0)[None, :, None]), 0.0) * ones
    crow = np.where(valid[None, :, None], np.exp(lg * (DEC_SEQ - 1.0 - t)[None, :, None]), 0.0) * ones
    gC = np.exp(_log_g() * DEC_SEQ)
    return (jnp.asarray(decay, F32), jnp.asarray(arow, F32), jnp.asarray(crow, F32), jnp.asarray(gC, F32))


def _norm_kernel(xp_ref, xs_ref, g_ref, hp_ref, hs_ref):
    i = pl.program_id(0)

    @pl.when(i < NP)
    def _():
        hp_ref[...] = _rms(xp_ref[...], g_ref[...]).astype(BF16)

    @pl.when(i == NP)
    def _():
        hs_ref[...] = _rms(xs_ref[...], g_ref[...]).astype(BF16)


def _norm(xp, xs, g):
    prow = pl.BlockSpec((TMP, D_MODEL), lambda i: (jnp.minimum(i, NP - 1), 0))
    srow = pl.BlockSpec((ROWS_S, D_MODEL), lambda i: (0, 0))
    return pl.pallas_call(
        _norm_kernel,
        grid=(NP + 1,),
        in_specs=[prow, srow, pl.BlockSpec((1, D_MODEL), lambda i: (0, 0))],
        out_specs=[prow, srow],
        out_shape=[jax.ShapeDtypeStruct((ROWS_P, D_MODEL), BF16),
                   jax.ShapeDtypeStruct((ROWS_S, D_MODEL), BF16)],
        compiler_params=_params(("arbitrary",)),
        name="norm",
    )(xp, xs, g)


def _hp_spec():
    return pl.BlockSpec((TMP, D_MODEL), lambda j, i: (jnp.minimum(i, NP - 1), 0))


def _hs_spec():
    return pl.BlockSpec((ROWS_S, D_MODEL), lambda j, i: (0, 0))


def _out_specs(tn):
    return [pl.BlockSpec((TMP, tn), lambda j, i: (jnp.minimum(i, NP - 1), j)),
            pl.BlockSpec((ROWS_S, tn), lambda j, i: (0, j))]


def _qk_kernel(hp_ref, hs_ref, w_ref, cp_ref, sp_ref, cs_ref, ss_ref, op_ref, os_ref, wb):
    j = pl.program_id(0)
    i = pl.program_id(1)

    @pl.when(i == 0)
    def _():
        wb[...] = w_ref[...].astype(BF16)

    scale = jnp.where(j < D_QK // TNW, 1.0, DK ** -0.5).astype(F32)
    half = DK // 2

    def run(h_ref, c_ref, s_ref, o_ref, rows):
        for r0 in range(0, rows, SUB):
            z = jnp.dot(h_ref[r0:r0 + SUB, :], wb[...], preferred_element_type=F32)
            c = c_ref[r0:r0 + SUB, :]
            s = s_ref[r0:r0 + SUB, :]
            for hh in range(TNW // DK):
                x1 = z[:, hh * DK:hh * DK + half]
                x2 = z[:, hh * DK + half:(hh + 1) * DK]
                o_ref[r0:r0 + SUB, hh * DK:hh * DK + half] = ((x1 * c - x2 * s) * scale).astype(BF16)
                o_ref[r0:r0 + SUB, hh * DK + half:(hh + 1) * DK] = ((x2 * c + x1 * s) * scale).astype(BF16)

    @pl.when(i < NP)
    def _():
        run(hp_ref, cp_ref, sp_ref, op_ref, TMP)

    @pl.when(i == NP)
    def _():
        run(hs_ref, cs_ref, ss_ref, os_ref, ROWS_S)


def _qk_proj(hp, hs, w_in, rope):
    cos_p, sin_p, cos_s, sin_s = rope
    ptab = pl.BlockSpec((TMP, DK // 2), lambda j, i: (i % TPS, 0))
    stab = pl.BlockSpec((ROWS_S, DK // 2), lambda j, i: (0, 0))
    return pl.pallas_call(
        _qk_kernel,
        grid=(2 * D_QK // TNW, NP + 1),
        in_specs=[_hp_spec(), _hs_spec(), pl.BlockSpec((D_MODEL, TNW), lambda j, i: (0, j)),
                  ptab, ptab, stab, stab],
        out_specs=_out_specs(TNW),
        out_shape=[jax.ShapeDtypeStruct((ROWS_P, 2 * D_QK), BF16),
                   jax.ShapeDtypeStruct((ROWS_S, 2 * D_QK), BF16)],
        scratch_shapes=[pltpu.VMEM((D_MODEL, TNW), BF16)],
        compiler_params=_params(("arbitrary", "arbitrary")),
        name="qk_proj",
    )(hp, hs, w_in, cos_p, sin_p, cos_s, sin_s)


def _plain_kernel(hp_ref, hs_ref, w_ref, op_ref, os_ref, wb):
    i = pl.program_id(1)

    @pl.when(i == 0)
    def _():
        wb[...] = w_ref[...].astype(BF16)

    def run(h_ref, o_ref, rows):
        for r0 in range(0, rows, SUB):
            z = jnp.dot(h_ref[r0:r0 + SUB, :], wb[...], preferred_element_type=F32)
            o_ref[r0:r0 + SUB, :] = z.astype(o_ref.dtype)

    @pl.when(i < NP)
    def _():
        run(hp_ref, op_ref, TMP)

    @pl.when(i == NP)
    def _():
        run(hs_ref, os_ref, ROWS_S)


def _plain_proj(hp, hs, w_in, wcol, ncol, dtype, name):
    return pl.pallas_call(
        _plain_kernel,
        grid=(ncol, NP + 1),
        in_specs=[_hp_spec(), _hs_spec(), pl.BlockSpec((D_MODEL, TNW), lambda j, i: (0, wcol(j)))],
        out_specs=_out_specs(TNW),
        out_shape=[jax.ShapeDtypeStruct((ROWS_P, ncol * TNW), dtype),
                   jax.ShapeDtypeStruct((ROWS_S, ncol * TNW), dtype)],
        scratch_shapes=[pltpu.VMEM((D_MODEL, TNW), BF16)],
        compiler_params=_params(("arbitrary", "arbitrary")),
        name=name,
    )(hp, hs, w_in)


def _shift_rows(u, c0, c1):
    rows = lax.broadcasted_iota(jnp.int32, u.shape, 0)
    r1 = pltpu.roll(u, 1, 0)
    r2 = pltpu.roll(u, 2, 0)
    u1 = jnp.where(rows == 0, c1, r1)
    u2 = jnp.where(rows == 0, c0, jnp.where(rows == 1, c1, r2))
    return u1, u2


def _shift_slabs(u, st):
    n = DEC_BATCH
    u1 = jnp.concatenate([st[n:2 * n], u[0:(DEC_SEQ - 1) * n]], axis=0)
    u2 = jnp.concatenate([st, u[0:(DEC_SEQ - 2) * n]], axis=0)
    return u1, u2


def _conv3(w, u, u1, u2):
    return w[0:1] * u2 + w[1:2] * u1 + w[2:3] * u


def _state_specs(width):
    return [pl.BlockSpec((BATCH, 2, TC), lambda j, i: (0, 0, j)),
            pl.BlockSpec((2 * DEC_BATCH, TC), lambda j, i: (0, j))]


def _state_shapes(width):
    return [jax.ShapeDtypeStruct((BATCH, 2, width), F32),
            jax.ShapeDtypeStruct((2 * DEC_BATCH, width), F32)]


def _convproj_kernel(hp_ref, hs_ref, wgb_ref, wgc_ref, whc_ref, cw_ref, st_ref,
                     cbp_ref, cbs_ref, stp_ref, sts_ref, wb, carry):
    i = pl.program_id(1)

    @pl.when(i == 0)
    def _():
        wb[0] = wgb_ref[...].astype(BF16)
        wb[1] = wgc_ref[...].astype(BF16)
        wb[2] = whc_ref[...].astype(BF16)

    w = cw_ref[...]

    def proj(h):
        gb = jnp.dot(h, wb[0], preferred_element_type=F32)
        gc = jnp.dot(h, wb[1], preferred_element_type=F32)
        hc = jnp.dot(h, wb[2], preferred_element_type=F32)
        return gb, gc * hc

    @pl.when(i < NP)
    def _():
        @pl.when(i % TPS == 0)
        def _():
            carry[...] = jnp.zeros_like(carry)

        for r0 in range(0, TMP, SUB):
            gb, u = proj(hp_ref[r0:r0 + SUB, :])
            u1, u2 = _shift_rows(u, carry[0:1, :], carry[1:2, :])
            cbp_ref[r0:r0 + SUB, :] = (gb * _conv3(w, u, u1, u2)).astype(BF16)
            carry[0:2, :] = u[SUB - 2:SUB, :]

        @pl.when(i % TPS == TPS - 1)
        def _():
            stp_ref[i // TPS] = carry[0:2, :]

    @pl.when(i == NP)
    def _():
        gb, u = proj(hs_ref[...])
        u1, u2 = _shift_slabs(u, st_ref[...])
        cbs_ref[...] = (gb * _conv3(w, u, u1, u2)).astype(BF16)
        sts_ref[...] = u[(DEC_SEQ - 2) * DEC_BATCH:, :]


def _conv_proj(hp, hs, w_in, conv_w, st):
    def wspec(col0):
        return pl.BlockSpec((D_MODEL, TC), lambda j, i: (0, col0 // TC + j))
    return pl.pallas_call(
        _convproj_kernel,
        grid=(D_CONV // TC, NP + 1),
        in_specs=[_hp_spec(), _hs_spec(), wspec(COL_GB), wspec(COL_GC), wspec(COL_HC),
                  pl.BlockSpec((3, TC), lambda j, i: (0, j)),
                  pl.BlockSpec((2 * DEC_BATCH, TC), lambda j, i: (0, j))],
        out_specs=_out_specs(TC) + _state_specs(D_CONV),
        out_shape=[jax.ShapeDtypeStruct((ROWS_P, D_CONV), BF16),
                   jax.ShapeDtypeStruct((ROWS_S, D_CONV), BF16)] + _state_shapes(D_CONV),
        scratch_shapes=[pltpu.VMEM((3, D_MODEL, TC), BF16), pltpu.VMEM((8, TC), F32)],
        compiler_params=_params(("arbitrary", "arbitrary")),
        name="conv_proj",
    )(hp, hs, w_in, w_in, w_in, conv_w, st)


def _tile_lanes(a, n):
    return jnp.concatenate([a] * n, axis=1)


def _gate_norm(o, g):
    on = o * lax.rsqrt(jnp.mean(o * o, axis=-1, keepdims=True) + EPS)
    return (jax.nn.silu(g) * on).astype(BF16)


def _ret_p_kernel(gc_ref, q_ref, k_ref, v_ref, g_ref, dec_ref, ar_ref, cr_ref, og_ref, st_ref, s_scr):
    h = pl.program_id(1)
    gC = gc_ref[h]
    decay = dec_ref[0]
    arow = _tile_lanes(ar_ref[0], DV // 128)
    crow = _tile_lanes(cr_ref[0], DK // 128)
    s_scr[...] = jnp.zeros_like(s_scr)

    def chunk(c, carry):
        r = pl.multiple_of(c * CHUNK, CHUNK)
        q = q_ref[pl.ds(r, CHUNK), :]
        k = k_ref[pl.ds(r, CHUNK), :]
        v = v_ref[pl.ds(r, CHUNK), :]
        s_prev = s_scr[...]
        sc = lax.dot_general(q, k, (((1,), (1,)), ((), ())), preferred_element_type=F32) * decay
        inner = jnp.dot(sc.astype(BF16), v, preferred_element_type=F32)
        cross = jnp.dot(q, s_prev.astype(BF16), preferred_element_type=F32) * arow
        kdt = (k.astype(F32) * crow).T.astype(BF16)
        s_scr[...] = gC * s_prev + jnp.dot(kdt, v, preferred_element_type=F32)
        og_ref[pl.ds(r, CHUNK), :] = _gate_norm(inner + cross, g_ref[pl.ds(r, CHUNK), :])
        return carry

    lax.fori_loop(0, SEQ // CHUNK, chunk, 0)
    st_ref[0, 0] = s_scr[...]


def _ret_prompt(qk, v, gg, tables):
    decay, arow, crow, gC = tables
    tab = pl.BlockSpec((1, CHUNK, 128), lambda b, h: (h, 0, 0))
    return pl.pallas_call(
        _ret_p_kernel,
        grid=(BATCH, N_HEADS),
        in_specs=[
            pl.BlockSpec(memory_space=pltpu.SMEM),
            pl.BlockSpec((SEQ, DK), lambda b, h: (b, h)),
            pl.BlockSpec((SEQ, DK), lambda b, h: (b, N_HEADS + h)),
            pl.BlockSpec((SEQ, DV), lambda b, h: (b, h)),
            pl.BlockSpec((SEQ, DV), lambda b, h: (b, h)),
            tab, tab, tab,
        ],
        out_specs=[
            pl.BlockSpec((SEQ, DV), lambda b, h: (b, h)),
            pl.BlockSpec((1, 1, DK, DV), lambda b, h: (b, h, 0, 0)),
        ],
        out_shape=[
            jax.ShapeDtypeStruct((ROWS_P, D_V), BF16),
            jax.ShapeDtypeStruct((BATCH, N_HEADS, DK, DV), F32),
        ],
        scratch_shapes=[pltpu.VMEM((DK, DV), F32)],
        compiler_params=_params(("arbitrary", "arbitrary")),
        name="ret_prompt",
    )(gC, qk, qk, v, gg, decay, arow, crow)


def _ret_d_kernel(gc_ref, q_ref, k_ref, v_ref, g_ref, s_ref, dec_ref, ar_ref, cr_ref,
                  og_ref, sn_ref, cross_scr, kdt_scr):
    h = pl.program_id(1)
    gC = gc_ref[h]
    q = q_ref[...]
    k = k_ref[...]
    v = v_ref[...]
    vb = v.astype(BF16)
    sc = lax.dot_general(q.astype(BF16), k.astype(BF16), (((1,), (1,)), ((), ())),
                         preferred_element_type=F32) * dec_ref[0]
    inner = jnp.dot(sc.astype(BF16), vb, preferred_element_type=F32)
    kdt_scr[...] = (k * _tile_lanes(cr_ref[0], DK // 128)).T.astype(BF16)
    row_seq = lax.shift_right_logical(lax.broadcasted_iota(jnp.int32, v.shape, 0), DEC_PAD.bit_length() - 1)

    def body(bi, carry):
        s_prev = s_ref[0, bi, 0]
        r = pl.multiple_of(bi * DEC_PAD, DEC_PAD)
        qs = q_ref[pl.ds(r, DEC_PAD), :].astype(BF16)
        cross_scr[pl.ds(r, DEC_PAD), :] = jnp.dot(qs, s_prev.astype(BF16), preferred_element_type=F32)
        vm = jnp.where(row_seq == bi, v, 0.0).astype(BF16)
        sn_ref[0, bi, 0] = gC * s_prev + jnp.dot(kdt_scr[...], vm, preferred_element_type=F32)
        return carry

    lax.fori_loop(0, DEC_GROUP, body, 0)
    o = inner + cross_scr[...] * _tile_lanes(ar_ref[0], DV // 128)
    og_ref[...] = _gate_norm(o, g_ref[...])


def _ret_decode(q, k, v, g, state, tables):
    decay, arow, crow, gC = tables
    ngrp = DEC_BATCH // DEC_GROUP
    tab = pl.BlockSpec((1, 128, 128), lambda b, h: (h, 0, 0))
    sspec = pl.BlockSpec((1, DEC_GROUP, 1, DK, DV), lambda b, h: (0, b, h, 0, 0))
    return pl.pallas_call(
        _ret_d_kernel,
        grid=(ngrp, N_HEADS),
        in_specs=[
            pl.BlockSpec(memory_space=pltpu.SMEM),
            pl.BlockSpec((128, DK), lambda b, h: (b, h)),
            pl.BlockSpec((128, DK), lambda b, h: (b, h)),
            pl.BlockSpec((128, DV), lambda b, h: (b, h)),
            pl.BlockSpec((128, DV), lambda b, h: (b, h)),
            sspec, tab, tab, tab,
        ],
        out_specs=[pl.BlockSpec((128, DV), lambda b, h: (b, h)), sspec],
        out_shape=[
            jax.ShapeDtypeStruct((DEC_BATCH * DEC_PAD, D_V), BF16),
            jax.ShapeDtypeStruct(state.shape, F32),
        ],
        scratch_shapes=[pltpu.VMEM((128, DV), F32), pltpu.VMEM((DK, 128), BF16)],
        compiler_params=_params(("arbitrary", "arbitrary")),
        name="ret_decode",
    )(gC, q, k, v, g, state, decay, arow, crow)


def _merge_kernel(og_ref, cb_ref, ga_ref, gb_ref, pr_ref, pc_ref, m_ref):
    ya = jnp.dot(og_ref[...], pr_ref[...], preferred_element_type=F32)
    yb = jnp.dot(cb_ref[...], pc_ref[...], preferred_element_type=F32)
    m = jax.nn.sigmoid(ga_ref[...]) * ya + jax.nn.sigmoid(gb_ref[...]) * yb
    m_ref[...] = m.astype(BF16)


def _merge(og, cb, gg, p_ret, p_conv):
    rows = og.shape[0]
    ga_blk = D_V // D_MODEL
    return pl.pallas_call(
        _merge_kernel,
        grid=(rows // TMR,),
        in_specs=[
            pl.BlockSpec((TMR, D_V), lambda i: (i, 0)),
            pl.BlockSpec((TMR, D_CONV), lambda i: (i, 0)),
            pl.BlockSpec((TMR, D_MODEL), lambda i: (i, ga_blk)),
            pl.BlockSpec((TMR, D_MODEL), lambda i: (i, ga_blk + 1)),
            _resident((D_V, D_MODEL)),
            _resident((D_CONV, D_MODEL)),
        ],
        out_specs=pl.BlockSpec((TMR, D_MODEL), lambda i: (i, 0)),
        out_shape=jax.ShapeDtypeStruct((rows, D_MODEL), BF16),
        compiler_params=_params(("arbitrary",)),
        name="merge",
    )(og, cb, gg, gg, p_ret, p_conv)


def _oproj_kernel(m_ref, x_ref, w_ref, gpost_ref, gpre_ref, x1_ref, h2_ref, wb):
    @pl.when(pl.program_id(0) == 0)
    def _():
        wb[...] = w_ref[...].astype(BF16)

    mix = jnp.dot(m_ref[...], wb[...], preferred_element_type=F32)
    x1 = x_ref[...] + _rms(mix, gpost_ref[...])
    x1_ref[...] = x1
    h2_ref[...] = _rms(x1, gpre_ref[...]).astype(BF16)


def _oproj(m, x, w_o, g_post, g_pre):
    rows = m.shape[0]
    row = pl.BlockSpec((TMO, D_MODEL), lambda i: (i, 0))
    return pl.pallas_call(
        _oproj_kernel,
        grid=(rows // TMO,),
        in_specs=[row, row, _resident((D_MODEL, D_MODEL)), _resident((1, D_MODEL)), _resident((1, D_MODEL))],
        out_specs=[row, row],
        out_shape=[jax.ShapeDtypeStruct((rows, D_MODEL), F32), jax.ShapeDtypeStruct((rows, D_MODEL), BF16)],
        scratch_shapes=[pltpu.VMEM((D_MODEL, D_MODEL), BF16)],
        compiler_params=_params(("arbitrary",)),
        name="oproj",
    )(m, x, w_o, g_post, g_pre)


def _ffn1_kernel(hp_ref, hs_ref, wu_ref, wg_ref, cw_ref, b_ref, st_ref,
                 ap_ref, as_ref, stp_ref, sts_ref, wb, carry):
    i = pl.program_id(1)

    @pl.when(i == 0)
    def _():
        wb[0] = wu_ref[...].astype(BF16)
        wb[1] = wg_ref[...].astype(BF16)

    w = cw_ref[...]
    b = b_ref[...]

    def proj(h):
        up = jnp.dot(h, wb[0], preferred_element_type=F32)
        gate = jnp.dot(h, wb[1], preferred_element_type=F32)
        return up, gate

    def act(up, u1, u2, gate):
        return (jax.nn.gelu(_conv3(w, up, u1, u2) + b, approximate=True) * gate).astype(BF16)

    @pl.when(i < NP)
    def _():
        @pl.when(i % TPS == 0)
        def _():
            carry[...] = jnp.zeros_like(carry)

        for r0 in range(0, TMP, SUB):
            up, gate = proj(hp_ref[r0:r0 + SUB, :])
            u1, u2 = _shift_rows(up, carry[0:1, :], carry[1:2, :])
            ap_ref[r0:r0 + SUB, :] = act(up, u1, u2, gate)
            carry[0:2, :] = up[SUB - 2:SUB, :]

        @pl.when(i % TPS == TPS - 1)
        def _():
            stp_ref[i // TPS] = carry[0:2, :]

    @pl.when(i == NP)
    def _():
        up, gate = proj(hs_ref[...])
        u1, u2 = _shift_slabs(up, st_ref[...])
        as_ref[...] = act(up, u1, u2, gate)
        sts_ref[...] = up[(DEC_SEQ - 2) * DEC_BATCH:, :]


def _ffn1(hp, hs, w_up, w_gate, conv_w, conv_b, st):
    wspec = pl.BlockSpec((D_MODEL, TC), lambda j, i: (0, j))
    return pl.pallas_call(
        _ffn1_kernel,
        grid=(D_FF // TC, NP + 1),
        in_specs=[_hp_spec(), _hs_spec(), wspec, wspec,
                  pl.BlockSpec((3, TC), lambda j, i: (0, j)),
                  pl.BlockSpec((1, TC), lambda j, i: (0, j)),
                  pl.BlockSpec((2 * DEC_BATCH, TC), lambda j, i: (0, j))],
        out_specs=_out_specs(TC) + _state_specs(D_FF),
        out_shape=[jax.ShapeDtypeStruct((ROWS_P, D_FF), BF16),
                   jax.ShapeDtypeStruct((ROWS_S, D_FF), BF16)] + _state_shapes(D_FF),
        scratch_shapes=[pltpu.VMEM((2, D_MODEL, TC), BF16), pltpu.VMEM((8, TC), F32)],
        compiler_params=_params(("arbitrary", "arbitrary")),
        name="ffn_up",
    )(hp, hs, w_up, w_gate, conv_w, conv_b, st)


def _ffn2_kernel(a_ref, x_ref, w_ref, g_ref, y_ref):
    f = jnp.dot(a_ref[...], w_ref[...], preferred_element_type=F32)
    y_ref[...] = x_ref[...] + _rms(f, g_ref[...])


def _ffn2(a, x, w_down, g_post):
    rows = a.shape[0]
    row = pl.BlockSpec((TMR, D_MODEL), lambda i: (i, 0))
    return pl.pallas_call(
        _ffn2_kernel,
        grid=(rows // TMR,),
        in_specs=[pl.BlockSpec((TMR, D_FF), lambda i: (i, 0)), row,
                  _resident((D_FF, D_MODEL)), _resident((1, D_MODEL))],
        out_specs=row,
        out_shape=jax.ShapeDtypeStruct((rows, D_MODEL), F32),
        compiler_params=_params(("arbitrary",)),
        name="ffn_down",
    )(a, x, w_down, g_post)


def _to_time_major(a):
    return a.transpose(1, 0, 2).reshape(a.shape[1] * a.shape[0], a.shape[2])


def _from_time_major(a, t):
    return a.reshape(t, DEC_BATCH, a.shape[1]).transpose(1, 0, 2)


def _to_padded_batch_major(a):
    a = _from_time_major(a, DEC_SEQ).astype(F32)
    a = jnp.pad(a, ((0, 0), (0, DEC_PAD - DEC_SEQ), (0, 0)))
    return a.reshape(DEC_BATCH * DEC_PAD, a.shape[2])


def _layer(xp, xs, s_ret, s_conv, s_ffn, rope, tabs_p, tabs_d,
           g_pre_mix, w_in, conv_w, p_ret, p_conv, w_o, g_post_mix,
           g_pre_ffn, w_up, w_gate, ffn_conv_w, ffn_conv_b, w_down, g_post_ffn):
    g_pre_mix = g_pre_mix.reshape(1, D_MODEL)
    g_post_mix = g_post_mix.reshape(1, D_MODEL)
    g_pre_ffn = g_pre_ffn.reshape(1, D_MODEL)
    g_post_ffn = g_post_ffn.reshape(1, D_MODEL)
    ffn_conv_b = ffn_conv_b.reshape(1, D_FF)
    p_ret, p_conv, w_down = (a.astype(BF16) for a in (p_ret, p_conv, w_down))

    hp, hs = _norm(xp, xs, g_pre_mix)
    qk_p, qk_s = _qk_proj(hp, hs, w_in, rope)
    v_p, v_s = _plain_proj(hp, hs, w_in, lambda j: COL_V // TNW + j, D_V // TNW, BF16, "v_proj")
    n_g = D_V // TNW
    gg_p, gg_s = _plain_proj(
        hp, hs, w_in, lambda j: jnp.where(j < n_g, COL_G // TNW + j, COL_GA // TNW - n_g + j),
        GG_W // TNW, F32, "gate_proj")
    cb_p, cb_s, conv_st_p, conv_st_s = _conv_proj(hp, hs, w_in, conv_w, _to_time_major(s_conv))

    og_p, ret_st_p = _ret_prompt(qk_p, v_p, gg_p, tabs_p)
    og_s8, ret_st_s = _ret_decode(
        _to_padded_batch_major(qk_s[:, :D_QK]), _to_padded_batch_major(qk_s[:, D_QK:]),
        _to_padded_batch_major(v_s), _to_padded_batch_major(gg_s[:, :D_V]),
        s_ret[None], tabs_d)
    og_s = _to_time_major(og_s8.reshape(DEC_BATCH, DEC_PAD, D_V)[:, :DEC_SEQ])

    m_p = _merge(og_p, cb_p, gg_p, p_ret, p_conv)
    m_s = _merge(og_s, cb_s, gg_s, p_ret, p_conv)
    x1_p, h2_p = _oproj(m_p, xp, w_o, g_post_mix, g_pre_ffn)
    x1_s, h2_s = _oproj(m_s, xs, w_o, g_post_mix, g_pre_ffn)

    a_p, a_s, ffn_st_p, ffn_st_s = _ffn1(h2_p, h2_s, w_up, w_gate, ffn_conv_w, ffn_conv_b,
                                         _to_time_major(s_ffn))
    y_p = _ffn2(a_p, x1_p, w_down, g_post_ffn)
    y_s = _ffn2(a_s, x1_s, w_down, g_post_ffn)
    return (y_p, y_s, ret_st_p, conv_st_p, ffn_st_p,
            ret_st_s[0], _from_time_major(conv_st_s, 2), _from_time_major(ffn_st_s, 2))


def kernel(x_prompt, x_sample, state_ret, state_conv, state_ffn, g_pre_mix, w_in, conv_w, p_ret, p_conv, w_o,
           g_post_mix, g_pre_ffn, w_up, w_gate, ffn_conv_w, ffn_conv_b, w_down, g_post_ffn):
    rope = _rope_tables()
    tabs_p = _ret_tables_prompt()
    tabs_d = _ret_tables_decode()
    depth = w_in.shape[0]
    xp = x_prompt.reshape(ROWS_P, D_MODEL)
    xs = _to_time_major(x_sample)
    rp, cp, fp, rs, cs, fs = [], [], [], [], [], []
    for l in range(depth):
        xp, xs, a, b, c, d, e, f = _layer(
            xp, xs, state_ret[l], state_conv[l], state_ffn[l], rope, tabs_p, tabs_d,
            g_pre_mix[l], w_in[l], conv_w[l], p_ret[l], p_conv[l], w_o[l], g_post_mix[l],
            g_pre_ffn[l], w_up[l], w_gate[l], ffn_conv_w[l], ffn_conv_b[l], w_down[l], g_post_ffn[l])
        rp.append(a); cp.append(b); fp.append(c)
        rs.append(d); cs.append(e); fs.append(f)
    y_prompt = xp.reshape(BATCH, SEQ, D_MODEL)
    y_sample = _from_time_major(xs, DEC_SEQ)
    return (y_prompt, y_sample, jnp.stack(rp), jnp.stack(cp), jnp.stack(fp),
            jnp.stack(rs), jnp.stack(cs), jnp.stack(fs))
```

```python
import functools

import numpy as np
import jax
import jax.numpy as jnp
from jax import lax
from jax.experimental import pallas as pl
from jax.experimental.pallas import tpu as pltpu

D_MODEL = 2048
BATCH = 4
SEQ = 2048
DEC_BATCH = 128
DEC_SEQ = 4
PAST_LEN = 16384
N_HEADS = 8
DK = D_MODEL // N_HEADS
DV = 2 * DK
D_QK = N_HEADS * DK
D_V = N_HEADS * DV
CHUNK = 256
ROPE_BASE = 10000.0
D_CONV = D_MODEL
D_FF = 5632
EPS = 1e-6
D_IN = 2 * D_QK + 2 * D_V + 3 * D_CONV + 2 * D_MODEL

F32 = jnp.float32
BF16 = jnp.bfloat16

ROWS_P = BATCH * SEQ
ROWS_S = DEC_SEQ * DEC_BATCH
TM_NORM = 1024
TILE_QK = (1024, 1024)
TILE_V = (1024, 512)
TILE_GATE = (1024, 512)
TILE_CONV = (1024, 512)
TILE_FFN = (1024, 512)
TNW = 1024
TC = 512
TMR = 256
TMO = 512
COL_V = 2 * D_QK
COL_G = COL_V + D_V
COL_GB = COL_G + D_V
COL_GC = COL_GB + D_CONV
COL_HC = COL_GC + D_CONV
COL_GA = COL_HC + D_CONV
GG_W = D_V + 2 * D_MODEL
DEC_GROUP = 16
DEC_LIVE = DEC_SEQ * DEC_GROUP
VMEM_LIMIT = 56 * 1024 * 1024


def _params(sem):
    return pltpu.CompilerParams(dimension_semantics=sem, vmem_limit_bytes=VMEM_LIMIT)


def _rms(x, g):
    ms = jnp.mean(x * x, axis=-1, keepdims=True)
    return x * lax.rsqrt(ms + EPS) * g


def _resident(shape):
    return pl.BlockSpec(shape, lambda *_: (0,) * len(shape), pipeline_mode=pl.Buffered(1))


def _log_g():
    return np.log1p(-np.exp2(-5.0 - np.arange(N_HEADS, dtype=np.float64)))


def _rope_tables():
    half = DK // 2
    inv = ROPE_BASE ** (-np.arange(half, dtype=np.float64) / half)
    pos_p = np.arange(SEQ, dtype=np.float64)
    pos_s = np.repeat(PAST_LEN + np.arange(DEC_SEQ, dtype=np.float64), DEC_BATCH)
    out = []
    for pos in (pos_p, pos_s):
        ang = pos[:, None] * inv[None, :]
        out += [jnp.asarray(np.cos(ang), F32), jnp.asarray(np.sin(ang), F32)]
    return out


def _ret_tables_prompt():
    lg = _log_g()[:, None, None]
    i = np.arange(CHUNK, dtype=np.float64)
    diff = i[:, None] - i[None, :]
    decay = np.where(diff >= 0, np.exp(lg * np.where(diff >= 0, diff, 0.0)), 0.0)
    ones = np.ones((1, 1, 128))
    arow = np.exp(lg * (i + 1.0)[None, :, None]) * ones
    crow = np.exp(lg * (CHUNK - 1.0 - i)[None, :, None]) * ones
    gC = np.exp(_log_g() * CHUNK)
    return (jnp.asarray(decay, F32), jnp.asarray(arow, F32), jnp.asarray(crow, F32), jnp.asarray(gC, F32))


def _ret_tables_decode():
    lg = _log_g()[:, None, None]
    r = np.arange(128)
    seq, t = r % DEC_GROUP, (r // DEC_GROUP).astype(np.float64)
    valid = r < DEC_LIVE
    diff = t[:, None] - t[None, :]
    ok = (seq[:, None] == seq[None, :]) & (diff >= 0) & valid[:, None] & valid[None, :]
    decay = np.where(ok[None], np.exp(lg * np.where(ok, diff, 0.0)[None]), 0.0)[:, :DEC_LIVE, :DEC_LIVE]
    ones = np.ones((1, 1, 128))
    arow = (np.exp(lg * (t + 1.0)[None, :, None]) * ones)[:, :DEC_LIVE]
    crow = (np.exp(lg * (DEC_SEQ - 1.0 - t)[None, :, None]) * ones)[:, :DEC_LIVE]
    gC = np.exp(_log_g() * DEC_SEQ)
    return (jnp.asarray(decay, F32), jnp.asarray(arow, F32), jnp.asarray(crow, F32), jnp.asarray(gC, F32))


def _norm_kernel(xp_ref, xs_ref, g_ref, hp_ref, hs_ref):
    i = pl.program_id(0)
    NP = ROWS_P // TM_NORM

    @pl.when(i < NP)
    def _():
        hp_ref[...] = _rms(xp_ref[...], g_ref[...]).astype(BF16)

    @pl.when(i == NP)
    def _():
        hs_ref[...] = _rms(xs_ref[...], g_ref[...]).astype(BF16)


def _norm(xp, xs, g):
    NP = ROWS_P // TM_NORM
    prow = pl.BlockSpec((TM_NORM, D_MODEL), lambda i: (jnp.minimum(i, NP - 1), 0))
    srow = pl.BlockSpec((ROWS_S, D_MODEL), lambda i: (0, 0))
    return pl.pallas_call(
        _norm_kernel,
        grid=(NP + 1,),
        in_specs=[prow, srow, pl.BlockSpec((1, D_MODEL), lambda i: (0, 0))],
        out_specs=[prow, srow],
        out_shape=[jax.ShapeDtypeStruct((ROWS_P, D_MODEL), BF16),
                   jax.ShapeDtypeStruct((ROWS_S, D_MODEL), BF16)],
        compiler_params=_params(("arbitrary",)),
        name="norm",
    )(xp, xs, g)


def _hp_spec(tmp):
    return pl.BlockSpec((tmp, D_MODEL), lambda j, i: (jnp.minimum(i, ROWS_P // tmp - 1), 0))


def _hs_spec():
    return pl.BlockSpec((ROWS_S, D_MODEL), lambda j, i: (0, 0))


def _ahead(j, i, ncol):
    return jnp.minimum(j + jnp.minimum(i, 1), ncol - 1)


def _out_specs(tmp, tn):
    return [pl.BlockSpec((tmp, tn), lambda j, i: (jnp.minimum(i, ROWS_P // tmp - 1), j)),
            pl.BlockSpec((ROWS_S, tn), lambda j, i: (0, j))]


def _qk_kernel(tile, hp_ref, hs_ref, w_ref, cp_ref, sp_ref, cs_ref, ss_ref, op_ref, os_ref, wb):
    TMP, SUB = tile
    NP = ROWS_P // TMP
    j = pl.program_id(0)
    i = pl.program_id(1)

    @pl.when(i == 0)
    def _():
        wb[...] = w_ref[...].astype(BF16)

    scale = jnp.where(j < D_QK // TNW, 1.0, DK ** -0.5).astype(F32)
    half = DK // 2

    def run(h_ref, c_ref, s_ref, o_ref, rows):
        sub = min(SUB, rows)
        for r0 in range(0, rows, sub):
            z = jnp.dot(h_ref[r0:r0 + sub, :], wb[...], preferred_element_type=F32)
            c = c_ref[r0:r0 + sub, :]
            s = s_ref[r0:r0 + sub, :]
            for hh in range(TNW // DK):
                x1 = z[:, hh * DK:hh * DK + half]
                x2 = z[:, hh * DK + half:(hh + 1) * DK]
                o_ref[r0:r0 + sub, hh * DK:hh * DK + half] = ((x1 * c - x2 * s) * scale).astype(BF16)
                o_ref[r0:r0 + sub, hh * DK + half:(hh + 1) * DK] = ((x2 * c + x1 * s) * scale).astype(BF16)

    @pl.when(i < NP)
    def _():
        run(hp_ref, cp_ref, sp_ref, op_ref, TMP)

    @pl.when(i == NP)
    def _():
        run(hs_ref, cs_ref, ss_ref, os_ref, ROWS_S)


def _qk_proj(hp, hs, w_in, rope):
    cos_p, sin_p, cos_s, sin_s = rope
    TMP = TILE_QK[0]
    NP, TPS = ROWS_P // TMP, SEQ // TMP
    ptab = pl.BlockSpec((TMP, DK // 2), lambda j, i: (i % TPS, 0))
    stab = pl.BlockSpec((ROWS_S, DK // 2), lambda j, i: (0, 0))
    return pl.pallas_call(
        functools.partial(_qk_kernel, TILE_QK),
        grid=(2 * D_QK // TNW, NP + 1),
        in_specs=[_hp_spec(TMP), _hs_spec(),
                  pl.BlockSpec((D_MODEL, TNW), lambda j, i: (0, _ahead(j, i, 2 * D_QK // TNW))),
                  ptab, ptab, stab, stab],
        out_specs=_out_specs(TMP, TNW),
        out_shape=[jax.ShapeDtypeStruct((ROWS_P, 2 * D_QK), BF16),
                   jax.ShapeDtypeStruct((ROWS_S, 2 * D_QK), BF16)],
        scratch_shapes=[pltpu.VMEM((D_MODEL, TNW), BF16)],
        compiler_params=_params(("arbitrary", "arbitrary")),
        name="qk_proj",
    )(hp, hs, w_in, cos_p, sin_p, cos_s, sin_s)


def _plain_kernel(tile, n_silu, hp_ref, hs_ref, w_ref, op_ref, os_ref, wb):
    TMP, SUB = tile
    NP = ROWS_P // TMP
    j = pl.program_id(0)
    i = pl.program_id(1)

    @pl.when(i == 0)
    def _():
        wb[...] = w_ref[...].astype(BF16)

    def run(h_ref, o_ref, rows):
        sub = min(SUB, rows)
        for r0 in range(0, rows, sub):
            z = jnp.dot(h_ref[r0:r0 + sub, :], wb[...], preferred_element_type=F32)
            if n_silu is not None:
                z = jax.nn.sigmoid(z) * jnp.where(j < n_silu, z, 1.0)
            o_ref[r0:r0 + sub, :] = z.astype(o_ref.dtype)

    @pl.when(i < NP)
    def _():
        run(hp_ref, op_ref, TMP)

    @pl.when(i == NP)
    def _():
        run(hs_ref, os_ref, ROWS_S)


def _plain_proj(hp, hs, w_in, wcol, ncol, tile, name, n_silu=None):
    TMP = tile[0]
    return pl.pallas_call(
        functools.partial(_plain_kernel, tile, n_silu),
        grid=(ncol, ROWS_P // TMP + 1),
        in_specs=[_hp_spec(TMP), _hs_spec(),
                  pl.BlockSpec((D_MODEL, TNW), lambda j, i: (0, wcol(_ahead(j, i, ncol))))],
        out_specs=_out_specs(TMP, TNW),
        out_shape=[jax.ShapeDtypeStruct((ROWS_P, ncol * TNW), BF16),
                   jax.ShapeDtypeStruct((ROWS_S, ncol * TNW), BF16)],
        scratch_shapes=[pltpu.VMEM((D_MODEL, TNW), BF16)],
        compiler_params=_params(("arbitrary", "arbitrary")),
        name=name,
    )(hp, hs, w_in)


def _shift_rows(u, c0, c1):
    rows = lax.broadcasted_iota(jnp.int32, u.shape, 0)
    r1 = pltpu.roll(u, 1, 0)
    r2 = pltpu.roll(u, 2, 0)
    u1 = jnp.where(rows == 0, c1, r1)
    u2 = jnp.where(rows == 0, c0, jnp.where(rows == 1, c1, r2))
    return u1, u2


def _shift_slabs(u, st):
    n = DEC_BATCH
    u1 = jnp.concatenate([st[n:2 * n], u[0:(DEC_SEQ - 1) * n]], axis=0)
    u2 = jnp.concatenate([st, u[0:(DEC_SEQ - 2) * n]], axis=0)
    return u1, u2


def _conv3(w, u, u1, u2):
    return w[0:1] * u2 + w[1:2] * u1 + w[2:3] * u


def _state_specs(width):
    return [pl.BlockSpec((BATCH, 2, TC), lambda j, i: (0, 0, j)),
            pl.BlockSpec((2 * DEC_BATCH, TC), lambda j, i: (0, j))]


def _state_shapes(width):
    return [jax.ShapeDtypeStruct((BATCH, 2, width), F32),
            jax.ShapeDtypeStruct((2 * DEC_BATCH, width), F32)]


def _convproj_kernel(tile, hp_ref, hs_ref, wgb_ref, wgc_ref, whc_ref, cw_ref, st_ref,
                     cbp_ref, cbs_ref, stp_ref, sts_ref, wb, carry):
    TMP, SUB = tile
    NP, TPS = ROWS_P // TMP, SEQ // TMP
    i = pl.program_id(1)

    @pl.when(i == 0)
    def _():
        wb[0] = wgb_ref[...].astype(BF16)
        wb[1] = wgc_ref[...].astype(BF16)
        wb[2] = whc_ref[...].astype(BF16)

    w = cw_ref[...]

    def proj(h):
        gb = jnp.dot(h, wb[0], preferred_element_type=F32)
        gc = jnp.dot(h, wb[1], preferred_element_type=F32)
        hc = jnp.dot(h, wb[2], preferred_element_type=F32)
        return gb, gc * hc

    @pl.when(i < NP)
    def _():
        @pl.when(i % TPS == 0)
        def _():
            carry[...] = jnp.zeros_like(carry)

        for r0 in range(0, TMP, SUB):
            gb, u = proj(hp_ref[r0:r0 + SUB, :])
            u1, u2 = _shift_rows(u, carry[0:1, :], carry[1:2, :])
            cbp_ref[r0:r0 + SUB, :] = (gb * _conv3(w, u, u1, u2)).astype(BF16)
            carry[0:2, :] = u[SUB - 2:SUB, :]

        @pl.when(i % TPS == TPS - 1)
        def _():
            stp_ref[i // TPS] = carry[0:2, :]

    @pl.when(i == NP)
    def _():
        gb, u = proj(hs_ref[...])
        u1, u2 = _shift_slabs(u, st_ref[...])
        cbs_ref[...] = (gb * _conv3(w, u, u1, u2)).astype(BF16)
        sts_ref[...] = u[(DEC_SEQ - 2) * DEC_BATCH:, :]


def _conv_proj(hp, hs, w_in, conv_w, st):
    def wspec(col0):
        return pl.BlockSpec((D_MODEL, TC), lambda j, i: (0, col0 // TC + _ahead(j, i, D_CONV // TC)))
    TMP = TILE_CONV[0]
    return pl.pallas_call(
        functools.partial(_convproj_kernel, TILE_CONV),
        grid=(D_CONV // TC, ROWS_P // TMP + 1),
        in_specs=[_hp_spec(TMP), _hs_spec(), wspec(COL_GB), wspec(COL_GC), wspec(COL_HC),
                  pl.BlockSpec((3, TC), lambda j, i: (0, j)),
                  pl.BlockSpec((2 * DEC_BATCH, TC), lambda j, i: (0, j))],
        out_specs=_out_specs(TMP, TC) + _state_specs(D_CONV),
        out_shape=[jax.ShapeDtypeStruct((ROWS_P, D_CONV), BF16),
                   jax.ShapeDtypeStruct((ROWS_S, D_CONV), BF16)] + _state_shapes(D_CONV),
        scratch_shapes=[pltpu.VMEM((3, D_MODEL, TC), BF16), pltpu.VMEM((8, TC), F32)],
        compiler_params=_params(("arbitrary", "arbitrary")),
        name="conv_proj",
    )(hp, hs, w_in, w_in, w_in, conv_w, st)


def _tile_lanes(a, n):
    return jnp.concatenate([a] * n, axis=1)


def _gate_norm(o, sg):
    on = o * lax.rsqrt(jnp.mean(o * o, axis=-1, keepdims=True) + EPS)
    return (sg.astype(F32) * on).astype(BF16)


def _ret_p_kernel(gc_ref, q_ref, k_ref, v_ref, g_ref, dec_ref, ar_ref, cr_ref, og_ref, st_ref, s_scr):
    h = pl.program_id(1)
    gC = gc_ref[h]
    decay = dec_ref[0]
    arow = _tile_lanes(ar_ref[0], DV // 128)
    crow = _tile_lanes(cr_ref[0], DK // 128)
    s_scr[...] = jnp.zeros_like(s_scr)

    def chunk(c, carry):
        r = pl.multiple_of(c * CHUNK, CHUNK)
        q = q_ref[pl.ds(r, CHUNK), :]
        k = k_ref[pl.ds(r, CHUNK), :]
        v = v_ref[pl.ds(r, CHUNK), :]
        s_prev = s_scr[...]
        sc = lax.dot_general(q, k, (((1,), (1,)), ((), ())), preferred_element_type=F32) * decay
        inner = jnp.dot(sc.astype(BF16), v, preferred_element_type=F32)
        cross = jnp.dot(q, s_prev.astype(BF16), preferred_element_type=F32) * arow
        kdt = (k.astype(F32) * crow).T.astype(BF16)
        s_scr[...] = gC * s_prev + jnp.dot(kdt, v, preferred_element_type=F32)
        og_ref[pl.ds(r, CHUNK), :] = _gate_norm(inner + cross, g_ref[pl.ds(r, CHUNK), :])
        return carry

    lax.fori_loop(0, SEQ // CHUNK, chunk, 0, unroll=2)
    st_ref[0, 0] = s_scr[...]


def _ret_prompt(qk, v, gg, tables):
    decay, arow, crow, gC = tables
    tab = pl.BlockSpec((1, CHUNK, 128), lambda b, h: (h, 0, 0))
    return pl.pallas_call(
        _ret_p_kernel,
        grid=(BATCH, N_HEADS),
        in_specs=[
            pl.BlockSpec(memory_space=pltpu.SMEM),
            pl.BlockSpec((SEQ, DK), lambda b, h: (b, h)),
            pl.BlockSpec((SEQ, DK), lambda b, h: (b, N_HEADS + h)),
            pl.BlockSpec((SEQ, DV), lambda b, h: (b, h)),
            pl.BlockSpec((SEQ, DV), lambda b, h: (b, h)),
            pl.BlockSpec((1, CHUNK, CHUNK), lambda b, h: (h, 0, 0)), tab, tab,
        ],
        out_specs=[
            pl.BlockSpec((SEQ, DV), lambda b, h: (b, h)),
            pl.BlockSpec((1, 1, DK, DV), lambda b, h: (b, h, 0, 0)),
        ],
        out_shape=[
            jax.ShapeDtypeStruct((ROWS_P, D_V), BF16),
            jax.ShapeDtypeStruct((BATCH, N_HEADS, DK, DV), F32),
        ],
        scratch_shapes=[pltpu.VMEM((DK, DV), F32)],
        compiler_params=_params(("arbitrary", "arbitrary")),
        name="ret_prompt",
    )(gC, qk, qk, v, gg, decay, arow, crow)


def _ret_d_kernel(gc_ref, q_ref, k_ref, v_ref, g_ref, s_ref, dec_ref, ar_ref, cr_ref,
                  og_ref, sn_ref, kdt_scr):
    h = pl.program_id(0)
    g0 = pl.multiple_of(pl.program_id(1) * DEC_GROUP, DEC_GROUP)
    gC = gc_ref[h]

    def rows(t):
        return pl.ds(t * DEC_BATCH + g0, DEC_GROUP)

    def gather(ref):
        return jnp.concatenate([ref[rows(t), :] for t in range(DEC_SEQ)], axis=0)

    q = gather(q_ref)
    k = gather(k_ref)
    v = gather(v_ref)
    sc = lax.dot_general(q, k, (((1,), (1,)), ((), ())), preferred_element_type=F32) * dec_ref[0]
    inner = jnp.dot(sc.astype(BF16), v, preferred_element_type=F32)
    kd = k.astype(F32) * _tile_lanes(cr_ref[0], DK // 128)
    kdt_scr[...] = jnp.concatenate([kd, jnp.zeros_like(kd)], axis=0).T.astype(BF16)
    row_seq = lax.broadcasted_iota(jnp.int32, v.shape, 0) & (DEC_GROUP - 1)
    vpad = jnp.zeros_like(v)

    def body(bi, cross):
        s_prev = s_ref[0, bi, 0]
        mine = row_seq == bi
        res = jnp.dot(q, s_prev.astype(BF16), preferred_element_type=F32)
        vm = jnp.concatenate([jnp.where(mine, v, vpad), vpad], axis=0)
        sn_ref[0, bi, 0] = gC * s_prev + jnp.dot(kdt_scr[...], vm, preferred_element_type=F32)
        return cross + jnp.where(mine, res, 0.0)

    cross = lax.fori_loop(0, DEC_GROUP, body, jnp.zeros((DEC_LIVE, DV), F32))
    o = inner + cross * _tile_lanes(ar_ref[0], DV // 128)
    og = _gate_norm(o, gather(g_ref))
    for t in range(DEC_SEQ):
        og_ref[rows(t), :] = og[t * DEC_GROUP:(t + 1) * DEC_GROUP, :]


def _ret_decode(qk, v, gg, state, tables):
    decay, arow, crow, gC = tables
    tab = pl.BlockSpec((1, DEC_LIVE, 128), lambda h, g: (h, 0, 0))
    sspec = pl.BlockSpec((1, DEC_GROUP, 1, DK, DV), lambda h, g: (0, g, h, 0, 0))
    return pl.pallas_call(
        _ret_d_kernel,
        grid=(N_HEADS, DEC_BATCH // DEC_GROUP),
        in_specs=[
            pl.BlockSpec(memory_space=pltpu.SMEM),
            pl.BlockSpec((ROWS_S, DK), lambda h, g: (0, h)),
            pl.BlockSpec((ROWS_S, DK), lambda h, g: (0, N_HEADS + h)),
            pl.BlockSpec((ROWS_S, DV), lambda h, g: (0, h)),
            pl.BlockSpec((ROWS_S, DV), lambda h, g: (0, h)),
            sspec,
            pl.BlockSpec((1, DEC_LIVE, DEC_LIVE), lambda h, g: (h, 0, 0)), tab, tab,
        ],
        out_specs=[pl.BlockSpec((ROWS_S, DV), lambda h, g: (0, h)), sspec],
        out_shape=[
            jax.ShapeDtypeStruct((ROWS_S, D_V), BF16),
            jax.ShapeDtypeStruct(state.shape, F32),
        ],
        scratch_shapes=[pltpu.VMEM((DK, 128), BF16)],
        compiler_params=_params(("arbitrary", "arbitrary")),
        name="ret_decode",
    )(gC, qk, qk, v, gg, state, decay, arow, crow)


def _merge_kernel(og_ref, cb_ref, ga_ref, gb_ref, pr_ref, pc_ref, m_ref):
    ya = jnp.dot(og_ref[...], pr_ref[...], preferred_element_type=F32)
    yb = jnp.dot(cb_ref[...], pc_ref[...], preferred_element_type=F32)
    m = ga_ref[...].astype(F32) * ya + gb_ref[...].astype(F32) * yb
    m_ref[...] = m.astype(BF16)


def _merge(og, cb, gg, p_ret, p_conv):
    rows = og.shape[0]
    ga_blk = D_V // D_MODEL
    return pl.pallas_call(
        _merge_kernel,
        grid=(rows // TMR,),
        in_specs=[
            pl.BlockSpec((TMR, D_V), lambda i: (i, 0)),
            pl.BlockSpec((TMR, D_CONV), lambda i: (i, 0)),
            pl.BlockSpec((TMR, D_MODEL), lambda i: (i, ga_blk)),
            pl.BlockSpec((TMR, D_MODEL), lambda i: (i, ga_blk + 1)),
            _resident((D_V, D_MODEL)),
            _resident((D_CONV, D_MODEL)),
        ],
        out_specs=pl.BlockSpec((TMR, D_MODEL), lambda i: (i, 0)),
        out_shape=jax.ShapeDtypeStruct((rows, D_MODEL), BF16),
        compiler_params=_params(("arbitrary",)),
        name="merge",
    )(og, cb, gg, gg, p_ret, p_conv)


def _oproj_kernel(m_ref, x_ref, w_ref, gpost_ref, gpre_ref, x1_ref, h2_ref, wb):
    @pl.when(pl.program_id(0) == 0)
    def _():
        wb[...] = w_ref[...].astype(BF16)

    mix = jnp.dot(m_ref[...], wb[...], preferred_element_type=F32)
    x1 = x_ref[...] + _rms(mix, gpost_ref[...])
    x1_ref[...] = x1
    h2_ref[...] = _rms(x1, gpre_ref[...]).astype(BF16)


def _oproj(m, x, w_o, g_post, g_pre):
    rows = m.shape[0]
    row = pl.BlockSpec((TMO, D_MODEL), lambda i: (i, 0))
    return pl.pallas_call(
        _oproj_kernel,
        grid=(rows // TMO,),
        in_specs=[row, row, _resident((D_MODEL, D_MODEL)), _resident((1, D_MODEL)), _resident((1, D_MODEL))],
        out_specs=[row, row],
        out_shape=[jax.ShapeDtypeStruct((rows, D_MODEL), F32), jax.ShapeDtypeStruct((rows, D_MODEL), BF16)],
        scratch_shapes=[pltpu.VMEM((D_MODEL, D_MODEL), BF16)],
        compiler_params=_params(("arbitrary",)),
        name="oproj",
    )(m, x, w_o, g_post, g_pre)


def _ffn1_kernel(tile, hp_ref, hs_ref, wu_ref, wg_ref, cw_ref, b_ref, st_ref,
                 ap_ref, as_ref, stp_ref, sts_ref, wb, carry):
    TMP, SUB = tile
    NP, TPS = ROWS_P // TMP, SEQ // TMP
    i = pl.program_id(1)

    @pl.when(i == 0)
    def _():
        wb[0] = wu_ref[...].astype(BF16)
        wb[1] = wg_ref[...].astype(BF16)

    w = cw_ref[...]
    b = b_ref[...]

    def proj(h):
        up = jnp.dot(h, wb[0], preferred_element_type=F32)
        gate = jnp.dot(h, wb[1], preferred_element_type=F32)
        return up, gate

    def act(up, u1, u2, gate):
        return (jax.nn.gelu(_conv3(w, up, u1, u2) + b, approximate=True) * gate).astype(BF16)

    @pl.when(i < NP)
    def _():
        @pl.when(i % TPS == 0)
        def _():
            carry[...] = jnp.zeros_like(carry)

        for r0 in range(0, TMP, SUB):
            up, gate = proj(hp_ref[r0:r0 + SUB, :])
            u1, u2 = _shift_rows(up, carry[0:1, :], carry[1:2, :])
            ap_ref[r0:r0 + SUB, :] = act(up, u1, u2, gate)
            carry[0:2, :] = up[SUB - 2:SUB, :]

        @pl.when(i % TPS == TPS - 1)
        def _():
            stp_ref[i // TPS] = carry[0:2, :]

    @pl.when(i == NP)
    def _():
        up, gate = proj(hs_ref[...])
        u1, u2 = _shift_slabs(up, st_ref[...])
        as_ref[...] = act(up, u1, u2, gate)
        sts_ref[...] = up[(DEC_SEQ - 2) * DEC_BATCH:, :]


def _ffn1(hp, hs, w_up, w_gate, conv_w, conv_b, st):
    wspec = pl.BlockSpec((D_MODEL, TC), lambda j, i: (0, _ahead(j, i, D_FF // TC)))
    TMP = TILE_FFN[0]
    return pl.pallas_call(
        functools.partial(_ffn1_kernel, TILE_FFN),
        grid=(D_FF // TC, ROWS_P // TMP + 1),
        in_specs=[_hp_spec(TMP), _hs_spec(), wspec, wspec,
                  pl.BlockSpec((3, TC), lambda j, i: (0, j)),
                  pl.BlockSpec((1, TC), lambda j, i: (0, j)),
                  pl.BlockSpec((2 * DEC_BATCH, TC), lambda j, i: (0, j))],
        out_specs=_out_specs(TMP, TC) + _state_specs(D_FF),
        out_shape=[jax.ShapeDtypeStruct((ROWS_P, D_FF), BF16),
                   jax.ShapeDtypeStruct((ROWS_S, D_FF), BF16)] + _state_shapes(D_FF),
        scratch_shapes=[pltpu.VMEM((2, D_MODEL, TC), BF16), pltpu.VMEM((8, TC), F32)],
        compiler_params=_params(("arbitrary", "arbitrary")),
        name="ffn_up",
    )(hp, hs, w_up, w_gate, conv_w, conv_b, st)


def _ffn2_kernel(a_ref, x_ref, w_ref, g_ref, y_ref):
    f = jnp.dot(a_ref[...], w_ref[...], preferred_element_type=F32)
    y_ref[...] = x_ref[...] + _rms(f, g_ref[...])


def _ffn2(a, x, w_down, g_post):
    rows = a.shape[0]
    row = pl.BlockSpec((TMR, D_MODEL), lambda i: (i, 0))
    return pl.pallas_call(
        _ffn2_kernel,
        grid=(rows // TMR,),
        in_specs=[pl.BlockSpec((TMR, D_FF), lambda i: (i, 0)), row,
                  _resident((D_FF, D_MODEL)), _resident((1, D_MODEL))],
        out_specs=row,
        out_shape=jax.ShapeDtypeStruct((rows, D_MODEL), F32),
        compiler_params=_params(("arbitrary",)),
        name="ffn_down",
    )(a, x, w_down, g_post)


def _to_time_major(a):
    return a.transpose(1, 0, 2).reshape(a.shape[1] * a.shape[0], a.shape[2])


def _from_time_major(a, t):
    return a.reshape(t, DEC_BATCH, a.shape[1]).transpose(1, 0, 2)


def _layer(xp, xs, s_ret, s_conv, s_ffn, rope, tabs_p, tabs_d,
           g_pre_mix, w_in, conv_w, p_ret, p_conv, w_o, g_post_mix,
           g_pre_ffn, w_up, w_gate, ffn_conv_w, ffn_conv_b, w_down, g_post_ffn):
    g_pre_mix = g_pre_mix.reshape(1, D_MODEL)
    g_post_mix = g_post_mix.reshape(1, D_MODEL)
    g_pre_ffn = g_pre_ffn.reshape(1, D_MODEL)
    g_post_ffn = g_post_ffn.reshape(1, D_MODEL)
    ffn_conv_b = ffn_conv_b.reshape(1, D_FF)
    p_ret, p_conv, w_down = (a.astype(BF16) for a in (p_ret, p_conv, w_down))

    hp, hs = _norm(xp, xs, g_pre_mix)
    qk_p, qk_s = _qk_proj(hp, hs, w_in, rope)
    v_p, v_s = _plain_proj(hp, hs, w_in, lambda j: COL_V // TNW + j, D_V // TNW, TILE_V, "v_proj")
    n_g = D_V // TNW
    gg_p, gg_s = _plain_proj(
        hp, hs, w_in, lambda j: jnp.where(j < n_g, COL_G // TNW + j, COL_GA // TNW - n_g + j),
        GG_W // TNW, TILE_GATE, "gate_proj", n_silu=n_g)
    cb_p, cb_s, conv_st_p, conv_st_s = _conv_proj(hp, hs, w_in, conv_w, _to_time_major(s_conv))

    og_p, ret_st_p = _ret_prompt(qk_p, v_p, gg_p, tabs_p)
    og_s, ret_st_s = _ret_decode(qk_s, v_s, gg_s, s_ret[None], tabs_d)

    m_p = _merge(og_p, cb_p, gg_p, p_ret, p_conv)
    m_s = _merge(og_s, cb_s, gg_s, p_ret, p_conv)
    x1_p, h2_p = _oproj(m_p, xp, w_o, g_post_mix, g_pre_ffn)
    x1_s, h2_s = _oproj(m_s, xs, w_o, g_post_mix, g_pre_ffn)

    a_p, a_s, ffn_st_p, ffn_st_s = _ffn1(h2_p, h2_s, w_up, w_gate, ffn_conv_w, ffn_conv_b,
                                         _to_time_major(s_ffn))
    y_p = _ffn2(a_p, x1_p, w_down, g_post_ffn)
    y_s = _ffn2(a_s, x1_s, w_down, g_post_ffn)
    return (y_p, y_s, ret_st_p, conv_st_p, ffn_st_p,
            ret_st_s[0], _from_time_major(conv_st_s, 2), _from_time_major(ffn_st_s, 2))


def kernel(x_prompt, x_sample, state_ret, state_conv, state_ffn, g_pre_mix, w_in, conv_w, p_ret, p_conv, w_o,
           g_post_mix, g_pre_ffn, w_up, w_gate, ffn_conv_w, ffn_conv_b, w_down, g_post_ffn):
    rope = _rope_tables()
    tabs_p = _ret_tables_prompt()
    tabs_d = _ret_tables_decode()
    depth = w_in.shape[0]
    xp = x_prompt.reshape(ROWS_P, D_MODEL)
    xs = _to_time_major(x_sample)
    rp, cp, fp, rs, cs, fs = [], [], [], [], [], []
    for l in range(depth):
        xp, xs, a, b, c, d, e, f = _layer(
            xp, xs, state_ret[l], state_conv[l], state_ffn[l], rope, tabs_p, tabs_d,
            g_pre_mix[l], w_in[l], conv_w[l], p_ret[l], p_conv[l], w_o[l], g_post_mix[l],
            g_pre_ffn[l], w_up[l], w_gate[l], ffn_conv_w[l], ffn_conv_b[l], w_down[l], g_post_ffn[l])
        rp.append(a); cp.append(b); fp.append(c)
        rs.append(d); cs.append(e); fs.append(f)
    y_prompt = xp.reshape(BATCH, SEQ, D_MODEL)
    y_sample = _from_time_major(xs, DEC_SEQ)
    return (y_prompt, y_sample, jnp.stack(rp), jnp.stack(cp), jnp.stack(fp),
            jnp.stack(rs), jnp.stack(cs), jnp.stack(fs))
```

```python
import functools

import numpy as np
import jax
import jax.numpy as jnp
from jax import lax
from jax.experimental import pallas as pl
from jax.experimental.pallas import tpu as pltpu

D_MODEL = 2048
BATCH = 4
SEQ = 2048
DEC_BATCH = 128
DEC_SEQ = 4
PAST_LEN = 16384
N_HEADS = 8
DK = D_MODEL // N_HEADS
DV = 2 * DK
D_QK = N_HEADS * DK
D_V = N_HEADS * DV
CHUNK = 256
ROPE_BASE = 10000.0
D_CONV = D_MODEL
D_FF = 5632
EPS = 1e-6
D_IN = 2 * D_QK + 2 * D_V + 3 * D_CONV + 2 * D_MODEL

F32 = jnp.float32
BF16 = jnp.bfloat16

ROWS_P = BATCH * SEQ
ROWS_S = DEC_SEQ * DEC_BATCH
ROWS = ROWS_P + ROWS_S
TMP = 1024
NP = ROWS_P // TMP
NT = NP + 1
TPS = SEQ // TMP
SUB_QK = 1024
SUB = 512
TNW = 1024
TC = 512
TMR = 256
NR_P = ROWS_P // TMR
NR = ROWS // TMR
WCH = 512
COL_V = 2 * D_QK
COL_G = COL_V + D_V
COL_GB = COL_G + D_V
COL_GC = COL_GB + D_CONV
COL_HC = COL_GC + D_CONV
COL_GA = COL_HC + D_CONV
GG_W = D_V + 2 * D_MODEL
DEC_GROUP = 16
DEC_LIVE = DEC_SEQ * DEC_GROUP
DEC_BLK = ROWS_P // ROWS_S
VMEM_LIMIT = 56 * 1024 * 1024


def _params(sem):
    return pltpu.CompilerParams(dimension_semantics=sem, vmem_limit_bytes=VMEM_LIMIT)


def _rms(x, g):
    ms = jnp.mean(x * x, axis=-1, keepdims=True)
    return x * lax.rsqrt(ms + EPS) * g


def _vec_spec(width):
    return pl.BlockSpec((1, width), lambda *_: (0, 0))


def _log_g():
    return np.log1p(-np.exp2(-5.0 - np.arange(N_HEADS, dtype=np.float64)))


def _rope_tables():
    half = DK // 2
    inv = ROPE_BASE ** (-np.arange(half, dtype=np.float64) / half)
    pos_p = np.arange(SEQ, dtype=np.float64)
    pos_s = np.repeat(PAST_LEN + np.arange(DEC_SEQ, dtype=np.float64), DEC_BATCH)
    out = []
    for pos in (pos_p, pos_s):
        ang = pos[:, None] * inv[None, :]
        out += [jnp.asarray(np.cos(ang), F32), jnp.asarray(np.sin(ang), F32)]
    return out


def _ret_tables_prompt():
    lg = _log_g()[:, None, None]
    i = np.arange(CHUNK, dtype=np.float64)
    diff = i[:, None] - i[None, :]
    decay = np.where(diff >= 0, np.exp(lg * np.where(diff >= 0, diff, 0.0)), 0.0)
    ones = np.ones((1, 1, 128))
    arow = np.exp(lg * (i + 1.0)[None, :, None]) * ones
    crow = np.exp(lg * (CHUNK - 1.0 - i)[None, :, None]) * ones
    gC = np.exp(_log_g() * CHUNK)
    return (jnp.asarray(decay, F32), jnp.asarray(arow, F32), jnp.asarray(crow, F32), jnp.asarray(gC, F32))


def _ret_tables_decode():
    lg = _log_g()[:, None, None]
    r = np.arange(128)
    seq, t = r % DEC_GROUP, (r // DEC_GROUP).astype(np.float64)
    valid = r < DEC_LIVE
    diff = t[:, None] - t[None, :]
    ok = (seq[:, None] == seq[None, :]) & (diff >= 0) & valid[:, None] & valid[None, :]
    decay = np.where(ok[None], np.exp(lg * np.where(ok, diff, 0.0)[None]), 0.0)[:, :DEC_LIVE, :DEC_LIVE]
    ones = np.ones((1, 1, 128))
    arow = (np.exp(lg * (t + 1.0)[None, :, None]) * ones)[:, :DEC_LIVE]
    crow = (np.exp(lg * (DEC_SEQ - 1.0 - t)[None, :, None]) * ones)[:, :DEC_LIVE]
    gC = np.exp(_log_g() * DEC_SEQ)
    return (jnp.asarray(decay, F32), jnp.asarray(arow, F32), jnp.asarray(crow, F32), jnp.asarray(gC, F32))


def _norm_kernel(xp_ref, xs_ref, g_ref, h_ref):
    i = pl.program_id(0)

    @pl.when(i < NP)
    def _():
        h_ref[...] = _rms(xp_ref[...], g_ref[...]).astype(BF16)

    @pl.when(i == NP)
    def _():
        h_ref[0:ROWS_S, :] = _rms(xs_ref[...], g_ref[...]).astype(BF16)


def _norm(xp, xs, g):
    return pl.pallas_call(
        _norm_kernel,
        grid=(NT,),
        in_specs=[pl.BlockSpec((TMP, D_MODEL), lambda i: (jnp.minimum(i, NP - 1), 0)),
                  pl.BlockSpec((ROWS_S, D_MODEL), lambda i: (0, 0)),
                  _vec_spec(D_MODEL)],
        out_specs=pl.BlockSpec((TMP, D_MODEL), lambda i: (i, 0)),
        out_shape=jax.ShapeDtypeStruct((ROWS, D_MODEL), BF16),
        compiler_params=_params(("arbitrary",)),
        name="norm",
    )(xp, xs, g)


def _h_spec():
    return pl.BlockSpec((TMP, D_MODEL), lambda j, i: (i, 0))


def _o_spec(tn):
    return pl.BlockSpec((TMP, tn), lambda j, i: (i, j))


def _ahead(j, i, ncol):
    return jnp.minimum(j + jnp.minimum(i, 1), ncol - 1)


def _row_blocks(rows, sub):
    sub = min(sub, rows)
    return [(r0, sub) for r0 in range(0, rows, sub)]


def _qk_kernel(h_ref, w_ref, cp_ref, sp_ref, cs_ref, ss_ref, o_ref, wb):
    j = pl.program_id(0)
    i = pl.program_id(1)

    @pl.when(i == 0)
    def _():
        wb[...] = w_ref[...].astype(BF16)

    scale = jnp.where(j < D_QK // TNW, 1.0, DK ** -0.5).astype(F32)
    half = DK // 2

    def run(rows, c_ref, s_ref):
        for r0, n in _row_blocks(rows, SUB_QK):
            z = jnp.dot(h_ref[r0:r0 + n, :], wb[...], preferred_element_type=F32)
            c = c_ref[r0:r0 + n, :]
            s = s_ref[r0:r0 + n, :]
            for hh in range(TNW // DK):
                x1 = z[:, hh * DK:hh * DK + half]
                x2 = z[:, hh * DK + half:(hh + 1) * DK]
                o_ref[r0:r0 + n, hh * DK:hh * DK + half] = ((x1 * c - x2 * s) * scale).astype(BF16)
                o_ref[r0:r0 + n, hh * DK + half:(hh + 1) * DK] = ((x2 * c + x1 * s) * scale).astype(BF16)

    @pl.when(i < NP)
    def _():
        run(TMP, cp_ref, sp_ref)

    @pl.when(i == NP)
    def _():
        run(ROWS_S, cs_ref, ss_ref)


def _qk_proj(h, w_in, rope):
    cos_p, sin_p, cos_s, sin_s = rope
    ncol = 2 * D_QK // TNW
    ptab = pl.BlockSpec((TMP, DK // 2), lambda j, i: (i % TPS, 0))
    stab = pl.BlockSpec((ROWS_S, DK // 2), lambda j, i: (0, 0))
    return pl.pallas_call(
        _qk_kernel,
        grid=(ncol, NT),
        in_specs=[_h_spec(), pl.BlockSpec((D_MODEL, TNW), lambda j, i: (0, _ahead(j, i, ncol))),
                  ptab, ptab, stab, stab],
        out_specs=_o_spec(TNW),
        out_shape=jax.ShapeDtypeStruct((ROWS, 2 * D_QK), BF16),
        scratch_shapes=[pltpu.VMEM((D_MODEL, TNW), BF16)],
        compiler_params=_params(("arbitrary", "arbitrary")),
        name="qk_proj",
    )(h, w_in, cos_p, sin_p, cos_s, sin_s)


def _plain_kernel(n_silu, h_ref, w_ref, o_ref, wb):
    j = pl.program_id(0)
    i = pl.program_id(1)

    @pl.when(i == 0)
    def _():
        wb[...] = w_ref[...].astype(BF16)

    def run(rows):
        for r0, n in _row_blocks(rows, SUB):
            z = jnp.dot(h_ref[r0:r0 + n, :], wb[...], preferred_element_type=F32)
            if n_silu is not None:
                z = jax.nn.sigmoid(z) * jnp.where(j < n_silu, z, 1.0)
            o_ref[r0:r0 + n, :] = z.astype(BF16)

    @pl.when(i < NP)
    def _():
        run(TMP)

    @pl.when(i == NP)
    def _():
        run(ROWS_S)


def _plain_proj(h, w_in, wcol, ncol, name, n_silu=None):
    return pl.pallas_call(
        functools.partial(_plain_kernel, n_silu),
        grid=(ncol, NT),
        in_specs=[_h_spec(), pl.BlockSpec((D_MODEL, TNW), lambda j, i: (0, wcol(_ahead(j, i, ncol))))],
        out_specs=_o_spec(TNW),
        out_shape=jax.ShapeDtypeStruct((ROWS, ncol * TNW), BF16),
        scratch_shapes=[pltpu.VMEM((D_MODEL, TNW), BF16)],
        compiler_params=_params(("arbitrary", "arbitrary")),
        name=name,
    )(h, w_in)


def _shift_rows(u, c0, c1):
    rows = lax.broadcasted_iota(jnp.int32, u.shape, 0)
    r1 = pltpu.roll(u, 1, 0)
    r2 = pltpu.roll(u, 2, 0)
    u1 = jnp.where(rows == 0, c1, r1)
    u2 = jnp.where(rows == 0, c0, jnp.where(rows == 1, c1, r2))
    return u1, u2


def _shift_slabs(u, st):
    n = DEC_BATCH
    u1 = jnp.concatenate([st[n:2 * n], u[0:(DEC_SEQ - 1) * n]], axis=0)
    u2 = jnp.concatenate([st, u[0:(DEC_SEQ - 2) * n]], axis=0)
    return u1, u2


def _conv3(w, u, u1, u2):
    return w[0:1] * u2 + w[1:2] * u1 + w[2:3] * u


def _state_specs():
    return [pl.BlockSpec((BATCH, 2, TC), lambda j, i: (0, 0, j)),
            pl.BlockSpec((2 * DEC_BATCH, TC), lambda j, i: (0, j))]


def _state_shapes(width):
    return [jax.ShapeDtypeStruct((BATCH, 2, width), F32),
            jax.ShapeDtypeStruct((2 * DEC_BATCH, width), F32)]


def _conv_rows(i, carry, stp_ref, tile_fn):
    @pl.when(i % TPS == 0)
    def _():
        carry[...] = jnp.zeros_like(carry)

    for r0, n in _row_blocks(TMP, SUB):
        u = tile_fn(r0, n, carry[0:1, :], carry[1:2, :])
        carry[0:2, :] = u[n - 2:n, :]

    @pl.when(i % TPS == TPS - 1)
    def _():
        stp_ref[i // TPS] = carry[0:2, :]


def _convproj_kernel(h_ref, wgb_ref, wgc_ref, whc_ref, cw_ref, st_ref, cb_ref, stp_ref, sts_ref, wb, carry):
    i = pl.program_id(1)

    @pl.when(i == 0)
    def _():
        wb[0] = wgb_ref[...].astype(BF16)
        wb[1] = wgc_ref[...].astype(BF16)
        wb[2] = whc_ref[...].astype(BF16)

    w = cw_ref[...]

    def proj(h):
        gb = jnp.dot(h, wb[0], preferred_element_type=F32)
        gc = jnp.dot(h, wb[1], preferred_element_type=F32)
        hc = jnp.dot(h, wb[2], preferred_element_type=F32)
        return gb, gc * hc

    def tile(r0, n, c0, c1):
        gb, u = proj(h_ref[r0:r0 + n, :])
        u1, u2 = _shift_rows(u, c0, c1)
        cb_ref[r0:r0 + n, :] = (gb * _conv3(w, u, u1, u2)).astype(BF16)
        return u

    @pl.when(i < NP)
    def _():
        _conv_rows(i, carry, stp_ref, tile)

    @pl.when(i == NP)
    def _():
        gb, u = proj(h_ref[0:ROWS_S, :])
        u1, u2 = _shift_slabs(u, st_ref[...])
        cb_ref[0:ROWS_S, :] = (gb * _conv3(w, u, u1, u2)).astype(BF16)
        sts_ref[...] = u[(DEC_SEQ - 2) * DEC_BATCH:, :]


def _conv_proj(h, w_in, conv_w, st):
    ncol = D_CONV // TC

    def wspec(col0):
        return pl.BlockSpec((D_MODEL, TC), lambda j, i: (0, col0 // TC + _ahead(j, i, ncol)))
    return pl.pallas_call(
        _convproj_kernel,
        grid=(ncol, NT),
        in_specs=[_h_spec(), wspec(COL_GB), wspec(COL_GC), wspec(COL_HC),
                  pl.BlockSpec((3, TC), lambda j, i: (0, j)),
                  pl.BlockSpec((2 * DEC_BATCH, TC), lambda j, i: (0, j))],
        out_specs=[_o_spec(TC)] + _state_specs(),
        out_shape=[jax.ShapeDtypeStruct((ROWS, D_CONV), BF16)] + _state_shapes(D_CONV),
        scratch_shapes=[pltpu.VMEM((3, D_MODEL, TC), BF16), pltpu.VMEM((8, TC), F32)],
        compiler_params=_params(("arbitrary", "arbitrary")),
        name="conv_proj",
    )(h, w_in, w_in, w_in, conv_w, st)


def _tile_lanes(a, n):
    return jnp.concatenate([a] * n, axis=1)


def _gate_norm(o, sg):
    on = o * lax.rsqrt(jnp.mean(o * o, axis=-1, keepdims=True) + EPS)
    return (sg.astype(F32) * on).astype(BF16)


def _ret_p_kernel(gc_ref, q_ref, k_ref, v_ref, g_ref, dec_ref, ar_ref, cr_ref, og_ref, st_ref, s_scr):
    h = pl.program_id(1)
    gC = gc_ref[h]
    decay = dec_ref[0]
    arow = _tile_lanes(ar_ref[0], DV // 128)
    crow = _tile_lanes(cr_ref[0], DK // 128)
    s_scr[...] = jnp.zeros_like(s_scr)

    def chunk(c, carry):
        r = pl.multiple_of(c * CHUNK, CHUNK)
        q = q_ref[pl.ds(r, CHUNK), :]
        k = k_ref[pl.ds(r, CHUNK), :]
        v = v_ref[pl.ds(r, CHUNK), :]
        s_prev = s_scr[...]
        sc = lax.dot_general(q, k, (((1,), (1,)), ((), ())), preferred_element_type=F32) * decay
        inner = jnp.dot(sc.astype(BF16), v, preferred_element_type=F32)
        cross = jnp.dot(q, s_prev.astype(BF16), preferred_element_type=F32) * arow
        kdt = (k.astype(F32) * crow).T.astype(BF16)
        s_scr[...] = gC * s_prev + jnp.dot(kdt, v, preferred_element_type=F32)
        og_ref[pl.ds(r, CHUNK), :] = _gate_norm(inner + cross, g_ref[pl.ds(r, CHUNK), :])
        return carry

    lax.fori_loop(0, SEQ // CHUNK, chunk, 0, unroll=4)
    st_ref[0, 0] = s_scr[...]


def _ret_prompt(qk, v, gg, tables):
    decay, arow, crow, gC = tables
    tab = pl.BlockSpec((1, CHUNK, 128), lambda b, h: (h, 0, 0))
    return pl.pallas_call(
        _ret_p_kernel,
        grid=(BATCH, N_HEADS),
        in_specs=[
            pl.BlockSpec(memory_space=pltpu.SMEM),
            pl.BlockSpec((SEQ, DK), lambda b, h: (b, h)),
            pl.BlockSpec((SEQ, DK), lambda b, h: (b, N_HEADS + h)),
            pl.BlockSpec((SEQ, DV), lambda b, h: (b, h)),
            pl.BlockSpec((SEQ, DV), lambda b, h: (b, h)),
            pl.BlockSpec((1, CHUNK, CHUNK), lambda b, h: (h, 0, 0)), tab, tab,
        ],
        out_specs=[
            pl.BlockSpec((SEQ, DV), lambda b, h: (b, h)),
            pl.BlockSpec((1, 1, DK, DV), lambda b, h: (b, h, 0, 0)),
        ],
        out_shape=[
            jax.ShapeDtypeStruct((ROWS, D_V), BF16),
            jax.ShapeDtypeStruct((BATCH, N_HEADS, DK, DV), F32),
        ],
        scratch_shapes=[pltpu.VMEM((DK, DV), F32)],
        compiler_params=_params(("arbitrary", "arbitrary")),
        name="ret_prompt",
    )(gC, qk, qk, v, gg, decay, arow, crow)


def _ret_d_kernel(gc_ref, og_in_ref, q_ref, k_ref, v_ref, g_ref, s_ref, dec_ref, ar_ref, cr_ref,
                  og_ref, sn_ref, kdt_scr):
    del og_in_ref
    h = pl.program_id(0)
    g0 = pl.multiple_of(pl.program_id(1) * DEC_GROUP, DEC_GROUP)
    gC = gc_ref[h]

    def rows(t):
        return pl.ds(t * DEC_BATCH + g0, DEC_GROUP)

    def gather(ref):
        return jnp.concatenate([ref[rows(t), :] for t in range(DEC_SEQ)], axis=0)

    q = gather(q_ref)
    k = gather(k_ref)
    v = gather(v_ref)
    sc = lax.dot_general(q, k, (((1,), (1,)), ((), ())), preferred_element_type=F32) * dec_ref[0]
    inner = jnp.dot(sc.astype(BF16), v, preferred_element_type=F32)
    kd = k.astype(F32) * _tile_lanes(cr_ref[0], DK // 128)
    kdt_scr[...] = jnp.concatenate([kd, jnp.zeros_like(kd)], axis=0).T.astype(BF16)
    row_seq = lax.broadcasted_iota(jnp.int32, v.shape, 0) & (DEC_GROUP - 1)
    vpad = jnp.zeros_like(v)

    def body(bi, cross):
        s_prev = s_ref[0, bi, 0]
        mine = row_seq == bi
        res = jnp.dot(q, s_prev.astype(BF16), preferred_element_type=F32)
        vm = jnp.concatenate([jnp.where(mine, v, vpad), vpad], axis=0)
        sn_ref[0, bi, 0] = gC * s_prev + jnp.dot(kdt_scr[...], vm, preferred_element_type=F32)
        return cross + jnp.where(mine, res, 0.0)

    cross = lax.fori_loop(0, DEC_GROUP, body, jnp.zeros((DEC_LIVE, DV), F32))
    o = inner + cross * _tile_lanes(ar_ref[0], DV // 128)
    og = _gate_norm(o, gather(g_ref))
    for t in range(DEC_SEQ):
        og_ref[rows(t), :] = og[t * DEC_GROUP:(t + 1) * DEC_GROUP, :]


def _ret_decode(og, qk, v, gg, state, tables):
    decay, arow, crow, gC = tables
    tab = pl.BlockSpec((1, DEC_LIVE, 128), lambda h, g: (h, 0, 0))
    sspec = pl.BlockSpec((1, DEC_GROUP, 1, DK, DV), lambda h, g: (0, g, h, 0, 0))
    return pl.pallas_call(
        _ret_d_kernel,
        grid=(N_HEADS, DEC_BATCH // DEC_GROUP),
        in_specs=[
            pl.BlockSpec(memory_space=pltpu.SMEM),
            pl.BlockSpec(memory_space=pl.ANY),
            pl.BlockSpec((ROWS_S, DK), lambda h, g: (DEC_BLK, h)),
            pl.BlockSpec((ROWS_S, DK), lambda h, g: (DEC_BLK, N_HEADS + h)),
            pl.BlockSpec((ROWS_S, DV), lambda h, g: (DEC_BLK, h)),
            pl.BlockSpec((ROWS_S, DV), lambda h, g: (DEC_BLK, h)),
            sspec,
            pl.BlockSpec((1, DEC_LIVE, DEC_LIVE), lambda h, g: (h, 0, 0)), tab, tab,
        ],
        out_specs=[pl.BlockSpec((ROWS_S, DV), lambda h, g: (DEC_BLK, h)), sspec],
        out_shape=[
            jax.ShapeDtypeStruct((ROWS, D_V), BF16),
            jax.ShapeDtypeStruct(state.shape, F32),
        ],
        input_output_aliases={1: 0},
        scratch_shapes=[pltpu.VMEM((DK, 128), BF16)],
        compiler_params=_params(("arbitrary", "arbitrary")),
        name="ret_decode",
    )(gC, og, qk, qk, v, gg, state, decay, arow, crow)


def _cast_chunk(s, n_chunks, w_ref, wb):
    @pl.when(s < n_chunks)
    def _():
        r = pl.multiple_of(s * WCH, WCH)
        wb[pl.ds(r, WCH), :] = w_ref[...].astype(BF16)


def _chunk_spec(rows):
    n = rows // WCH
    return pl.BlockSpec((WCH, D_MODEL), lambda s: (jnp.minimum(s, n - 1), 0))


def _tile_spec(ncast, width, col=0, tm=TMR):
    return pl.BlockSpec((tm, width), lambda s: (jnp.maximum(s - ncast, 0), col))


NCAST_MERGE = D_V // WCH


def _merge_kernel(og_ref, cb_ref, ga_ref, gb_ref, pr_ref, pc_ref, m_ref, prb, pcb):
    s = pl.program_id(0)
    _cast_chunk(s, D_V // WCH, pr_ref, prb)
    _cast_chunk(s, D_CONV // WCH, pc_ref, pcb)

    @pl.when(s >= NCAST_MERGE)
    def _():
        ya = jnp.dot(og_ref[...], prb[...], preferred_element_type=F32)
        yb = jnp.dot(cb_ref[...], pcb[...], preferred_element_type=F32)
        m = ga_ref[...].astype(F32) * ya + gb_ref[...].astype(F32) * yb
        m_ref[...] = m.astype(BF16)


def _merge(og, cb, gg, p_ret, p_conv):
    ga_blk = D_V // D_MODEL
    nc = NCAST_MERGE
    return pl.pallas_call(
        _merge_kernel,
        grid=(nc + NR,),
        in_specs=[_tile_spec(nc, D_V), _tile_spec(nc, D_CONV),
                  _tile_spec(nc, D_MODEL, ga_blk), _tile_spec(nc, D_MODEL, ga_blk + 1),
                  _chunk_spec(D_V), _chunk_spec(D_CONV)],
        out_specs=_tile_spec(nc, D_MODEL),
        out_shape=jax.ShapeDtypeStruct((ROWS, D_MODEL), BF16),
        scratch_shapes=[pltpu.VMEM((D_V, D_MODEL), BF16), pltpu.VMEM((D_CONV, D_MODEL), BF16)],
        compiler_params=_params(("arbitrary",)),
        name="merge",
    )(og, cb, gg, gg, p_ret, p_conv)


NCAST_O = D_MODEL // WCH
TMO = 256
NO_P = ROWS_P // TMO


def _oproj_kernel(m_ref, xp_ref, xs_ref, w_ref, gpost_ref, gpre_ref, x1_ref, h2_ref, wb):
    s = pl.program_id(0)
    _cast_chunk(s, NCAST_O, w_ref, wb)

    def finish(x_ref, r):
        x1 = x_ref[...] + r
        x1_ref[...] = x1
        h2_ref[...] = _rms(x1, gpre_ref[...]).astype(BF16)

    @pl.when(s >= NCAST_O)
    def _():
        mix = jnp.dot(m_ref[...], wb[...], preferred_element_type=F32)
        r = _rms(mix, gpost_ref[...])

        @pl.when(s < NCAST_O + NO_P)
        def _():
            finish(xp_ref, r)

        @pl.when(s >= NCAST_O + NO_P)
        def _():
            finish(xs_ref, r)


def _oproj(m, xp, xs, w_o, g_post, g_pre):
    nc = NCAST_O
    row = _tile_spec(nc, D_MODEL, tm=TMO)
    return pl.pallas_call(
        _oproj_kernel,
        grid=(nc + ROWS // TMO,),
        in_specs=[row,
                  pl.BlockSpec((TMO, D_MODEL), lambda s: (jnp.clip(s - nc, 0, NO_P - 1), 0)),
                  pl.BlockSpec((TMO, D_MODEL), lambda s: (jnp.maximum(s - nc - NO_P, 0), 0)),
                  _chunk_spec(D_MODEL), _vec_spec(D_MODEL), _vec_spec(D_MODEL)],
        out_specs=[row, row],
        out_shape=[jax.ShapeDtypeStruct((ROWS, D_MODEL), F32), jax.ShapeDtypeStruct((ROWS, D_MODEL), BF16)],
        scratch_shapes=[pltpu.VMEM((D_MODEL, D_MODEL), BF16)],
        compiler_params=_params(("arbitrary",)),
        name="oproj",
    )(m, xp, xs, w_o, g_post, g_pre)


def _ffn1_kernel(h_ref, wu_ref, wg_ref, cw_ref, b_ref, st_ref, a_ref, stp_ref, sts_ref, wb, carry):
    i = pl.program_id(1)

    @pl.when(i == 0)
    def _():
        wb[0] = wu_ref[...].astype(BF16)
        wb[1] = wg_ref[...].astype(BF16)

    w = cw_ref[...]
    b = b_ref[...]

    def proj(h):
        up = jnp.dot(h, wb[0], preferred_element_type=F32)
        gate = jnp.dot(h, wb[1], preferred_element_type=F32)
        return up, gate

    def act(up, u1, u2, gate):
        return (jax.nn.gelu(_conv3(w, up, u1, u2) + b, approximate=True) * gate).astype(BF16)

    def tile(r0, n, c0, c1):
        up, gate = proj(h_ref[r0:r0 + n, :])
        u1, u2 = _shift_rows(up, c0, c1)
        a_ref[r0:r0 + n, :] = act(up, u1, u2, gate)
        return up

    @pl.when(i < NP)
    def _():
        _conv_rows(i, carry, stp_ref, tile)

    @pl.when(i == NP)
    def _():
        up, gate = proj(h_ref[0:ROWS_S, :])
        u1, u2 = _shift_slabs(up, st_ref[...])
        a_ref[0:ROWS_S, :] = act(up, u1, u2, gate)
        sts_ref[...] = up[(DEC_SEQ - 2) * DEC_BATCH:, :]


def _ffn1(h, w_up, w_gate, conv_w, conv_b, st):
    ncol = D_FF // TC
    wspec = pl.BlockSpec((D_MODEL, TC), lambda j, i: (0, _ahead(j, i, ncol)))
    return pl.pallas_call(
        _ffn1_kernel,
        grid=(ncol, NT),
        in_specs=[_h_spec(), wspec, wspec,
                  pl.BlockSpec((3, TC), lambda j, i: (0, j)),
                  pl.BlockSpec((1, TC), lambda j, i: (0, j)),
                  pl.BlockSpec((2 * DEC_BATCH, TC), lambda j, i: (0, j))],
        out_specs=[_o_spec(TC)] + _state_specs(),
        out_shape=[jax.ShapeDtypeStruct((ROWS, D_FF), BF16)] + _state_shapes(D_FF),
        scratch_shapes=[pltpu.VMEM((2, D_MODEL, TC), BF16), pltpu.VMEM((8, TC), F32)],
        compiler_params=_params(("arbitrary", "arbitrary")),
        name="ffn_up",
    )(h, w_up, w_gate, conv_w, conv_b, st)


NCAST_DOWN = D_FF // WCH


def _ffn2_kernel(a_ref, x_ref, w_ref, g_ref, yp_ref, ys_ref, wb):
    s = pl.program_id(0)
    _cast_chunk(s, NCAST_DOWN, w_ref, wb)

    @pl.when(s >= NCAST_DOWN)
    def _():
        f = jnp.dot(a_ref[...], wb[...], preferred_element_type=F32)
        y = x_ref[...] + _rms(f, g_ref[...])

        @pl.when(s < NCAST_DOWN + NR_P)
        def _():
            yp_ref[...] = y

        @pl.when(s >= NCAST_DOWN + NR_P)
        def _():
            ys_ref[...] = y


def _ffn2(a, x, w_down, g_post):
    nc = NCAST_DOWN
    return pl.pallas_call(
        _ffn2_kernel,
        grid=(nc + NR,),
        in_specs=[_tile_spec(nc, D_FF), _tile_spec(nc, D_MODEL), _chunk_spec(D_FF), _vec_spec(D_MODEL)],
        out_specs=[pl.BlockSpec((TMR, D_MODEL), lambda s: (jnp.clip(s - nc, 0, NR_P - 1), 0)),
                   pl.BlockSpec((TMR, D_MODEL), lambda s: (jnp.maximum(s - nc - NR_P, 0), 0))],
        out_shape=[jax.ShapeDtypeStruct((ROWS_P, D_MODEL), F32), jax.ShapeDtypeStruct((ROWS_S, D_MODEL), F32)],
        scratch_shapes=[pltpu.VMEM((D_FF, D_MODEL), BF16)],
        compiler_params=_params(("arbitrary",)),
        name="ffn_down",
    )(a, x, w_down, g_post)


def _to_time_major(a):
    return a.transpose(1, 0, 2).reshape(a.shape[1] * a.shape[0], a.shape[2])


def _from_time_major(a, t):
    return a.reshape(t, DEC_BATCH, a.shape[1]).transpose(1, 0, 2)


def _layer(xp, xs, s_ret, s_conv, s_ffn, rope, tabs_p, tabs_d,
           g_pre_mix, w_in, conv_w, p_ret, p_conv, w_o, g_post_mix,
           g_pre_ffn, w_up, w_gate, ffn_conv_w, ffn_conv_b, w_down, g_post_ffn):
    g_pre_mix = g_pre_mix.reshape(1, D_MODEL)
    g_post_mix = g_post_mix.reshape(1, D_MODEL)
    g_pre_ffn = g_pre_ffn.reshape(1, D_MODEL)
    g_post_ffn = g_post_ffn.reshape(1, D_MODEL)
    ffn_conv_b = ffn_conv_b.reshape(1, D_FF)

    h = _norm(xp, xs, g_pre_mix)
    qk = _qk_proj(h, w_in, rope)
    v = _plain_proj(h, w_in, lambda j: COL_V // TNW + j, D_V // TNW, "v_proj")
    n_g = D_V // TNW
    gg = _plain_proj(h, w_in, lambda j: jnp.where(j < n_g, COL_G // TNW + j, COL_GA // TNW - n_g + j),
                     GG_W // TNW, "gate_proj", n_silu=n_g)
    cb, conv_st_p, conv_st_s = _conv_proj(h, w_in, conv_w, _to_time_major(s_conv))

    og, ret_st_p = _ret_prompt(qk, v, gg, tabs_p)
    og, ret_st_s = _ret_decode(og, qk, v, gg, s_ret[None], tabs_d)

    m = _merge(og, cb, gg, p_ret, p_conv)
    x1, h2 = _oproj(m, xp, xs, w_o, g_post_mix, g_pre_ffn)
    act, ffn_st_p, ffn_st_s = _ffn1(h2, w_up, w_gate, ffn_conv_w, ffn_conv_b, _to_time_major(s_ffn))
    y_p, y_s = _ffn2(act, x1, w_down, g_post_ffn)
    return (y_p, y_s, ret_st_p, conv_st_p, ffn_st_p,
            ret_st_s[0], _from_time_major(conv_st_s, 2), _from_time_major(ffn_st_s, 2))


def kernel(x_prompt, x_sample, state_ret, state_conv, state_ffn, g_pre_mix, w_in, conv_w, p_ret, p_conv, w_o,
           g_post_mix, g_pre_ffn, w_up, w_gate, ffn_conv_w, ffn_conv_b, w_down, g_post_ffn):
    rope = _rope_tables()
    tabs_p = _ret_tables_prompt()
    tabs_d = _ret_tables_decode()
    depth = w_in.shape[0]
    xp = x_prompt.reshape(ROWS_P, D_MODEL)
    xs = _to_time_major(x_sample)
    rp, cp, fp, rs, cs, fs = [], [], [], [], [], []
    for l in range(depth):
        xp, xs, a, b, c, d, e, f = _layer(
            xp, xs, state_ret[l], state_conv[l], state_ffn[l], rope, tabs_p, tabs_d,
            g_pre_mix[l], w_in[l], conv_w[l], p_ret[l], p_conv[l], w_o[l], g_post_mix[l],
            g_pre_ffn[l], w_up[l], w_gate[l], ffn_conv_w[l], ffn_conv_b[l], w_down[l], g_post_ffn[l])
        rp.append(a); cp.append(b); fp.append(c)
        rs.append(d); cs.append(e); fs.append(f)
    y_prompt = xp.reshape(BATCH, SEQ, D_MODEL)
    y_sample = _from_time_major(xs, DEC_SEQ)
    return (y_prompt, y_sample, jnp.stack(rp), jnp.stack(cp), jnp.stack(fp),
            jnp.stack(rs), jnp.stack(cs), jnp.stack(fs))
```

```python
import functools

import numpy as np
import jax
import jax.numpy as jnp
from jax import lax
from jax.experimental import pallas as pl
from jax.experimental.pallas import tpu as pltpu

D_MODEL = 2048
BATCH = 4
SEQ = 2048
DEC_BATCH = 128
DEC_SEQ = 4
PAST_LEN = 16384
N_HEADS = 8
DK = D_MODEL // N_HEADS
DV = 2 * DK
D_QK = N_HEADS * DK
D_V = N_HEADS * DV
CHUNK = 256
ROPE_BASE = 10000.0
D_CONV = D_MODEL
D_FF = 5632
EPS = 1e-6
D_IN = 2 * D_QK + 2 * D_V + 3 * D_CONV + 2 * D_MODEL

F32 = jnp.float32
BF16 = jnp.bfloat16

ROWS_P = BATCH * SEQ
ROWS_S = DEC_SEQ * DEC_BATCH
ROWS = ROWS_P + ROWS_S
TMP = 1024
NP = ROWS_P // TMP
NT = NP + 1
TPS = SEQ // TMP
SUB_QK = 1024
SUB = 512
TNW = 1024
TC = 512
TMR = 256
SLABS = TMR // DEC_BATCH
NR_P = ROWS_P // TMR
NR = ROWS // TMR
WCH = 512
COL_V = 2 * D_QK
COL_G = COL_V + D_V
COL_GB = COL_G + D_V
COL_GC = COL_GB + D_CONV
COL_HC = COL_GC + D_CONV
COL_GA = COL_HC + D_CONV
GG_W = D_V + 2 * D_MODEL
DEC_GROUP = 16
DEC_LIVE = DEC_SEQ * DEC_GROUP
DEC_BLK = ROWS_P // ROWS_S
VMEM_LIMIT = 56 * 1024 * 1024


def _params(sem):
    return pltpu.CompilerParams(dimension_semantics=sem, vmem_limit_bytes=VMEM_LIMIT)


def _rms(x, g):
    ms = jnp.mean(x * x, axis=-1, keepdims=True)
    return x * lax.rsqrt(ms + EPS) * g


def _vec_spec(width):
    return pl.BlockSpec((1, width), lambda *_: (0, 0))


def _log_g():
    return np.log1p(-np.exp2(-5.0 - np.arange(N_HEADS, dtype=np.float64)))


def _rope_tables():
    half = DK // 2
    inv = ROPE_BASE ** (-np.arange(half, dtype=np.float64) / half)
    pos_p = np.arange(SEQ, dtype=np.float64)
    pos_s = np.repeat(PAST_LEN + np.arange(DEC_SEQ, dtype=np.float64), DEC_BATCH)
    out = []
    for pos in (pos_p, pos_s):
        ang = pos[:, None] * inv[None, :]
        out += [jnp.asarray(np.cos(ang), F32), jnp.asarray(np.sin(ang), F32)]
    return out


def _ret_tables_prompt():
    lg = _log_g()[:, None, None]
    i = np.arange(CHUNK, dtype=np.float64)
    diff = i[:, None] - i[None, :]
    decay = np.where(diff >= 0, np.exp(lg * np.where(diff >= 0, diff, 0.0)), 0.0)
    ones = np.ones((1, 1, 128))
    arow = np.exp(lg * (i + 1.0)[None, :, None]) * ones
    crow = np.exp(lg * (CHUNK - 1.0 - i)[None, :, None]) * ones
    gC = np.exp(_log_g() * CHUNK)
    return (jnp.asarray(decay, F32), jnp.asarray(arow, F32), jnp.asarray(crow, F32), jnp.asarray(gC, F32))


def _ret_tables_decode():
    lg = _log_g()[:, None, None]
    r = np.arange(128)
    seq, t = r % DEC_GROUP, (r // DEC_GROUP).astype(np.float64)
    valid = r < DEC_LIVE
    diff = t[:, None] - t[None, :]
    ok = (seq[:, None] == seq[None, :]) & (diff >= 0) & valid[:, None] & valid[None, :]
    decay = np.where(ok[None], np.exp(lg * np.where(ok, diff, 0.0)[None]), 0.0)[:, :DEC_LIVE, :DEC_LIVE]
    ones = np.ones((1, 1, 128))
    arow = (np.exp(lg * (t + 1.0)[None, :, None]) * ones)[:, :DEC_LIVE]
    crow = (np.exp(lg * (DEC_SEQ - 1.0 - t)[None, :, None]) * ones)[:, :DEC_LIVE]
    gC = np.exp(_log_g() * DEC_SEQ)
    return (jnp.asarray(decay, F32), jnp.asarray(arow, F32), jnp.asarray(crow, F32), jnp.asarray(gC, F32))


def _norm_kernel(xp_ref, xs_ref, g_ref, h_ref):
    i = pl.program_id(0)

    @pl.when(i < NP)
    def _():
        h_ref[...] = _rms(xp_ref[...], g_ref[...]).astype(BF16)

    @pl.when(i == NP)
    def _():
        for t in range(DEC_SEQ):
            x = xs_ref[:, t * D_MODEL:(t + 1) * D_MODEL]
            h_ref[t * DEC_BATCH:(t + 1) * DEC_BATCH, :] = _rms(x, g_ref[...]).astype(BF16)


def _norm(xp, xs, g):
    return pl.pallas_call(
        _norm_kernel,
        grid=(NT,),
        in_specs=[pl.BlockSpec((TMP, D_MODEL), lambda i: (jnp.minimum(i, NP - 1), 0)),
                  pl.BlockSpec((DEC_BATCH, DEC_SEQ * D_MODEL), lambda i: (0, 0)),
                  _vec_spec(D_MODEL)],
        out_specs=pl.BlockSpec((TMP, D_MODEL), lambda i: (i, 0)),
        out_shape=jax.ShapeDtypeStruct((ROWS, D_MODEL), BF16),
        compiler_params=_params(("arbitrary",)),
        name="norm",
    )(xp, xs, g)


def _h_spec():
    return pl.BlockSpec((TMP, D_MODEL), lambda j, i: (i, 0))


def _o_spec(tn):
    return pl.BlockSpec((TMP, tn), lambda j, i: (i, j))


def _ahead(j, i, ncol):
    return jnp.minimum(j + jnp.minimum(i, 1), ncol - 1)


def _side_specs(n_chunks):
    def idx(j, i):
        return (jnp.clip(j * NT + i - 1, 0, n_chunks - 1), 0)
    return pl.BlockSpec((WCH, D_MODEL), idx), pl.BlockSpec((WCH, D_MODEL), idx)


def _side_cast(n_chunks, src_ref, dst_ref):
    t = pl.program_id(0) * NT + pl.program_id(1)

    @pl.when((t >= 1) & (t <= n_chunks))
    def _():
        dst_ref[...] = src_ref[...].astype(BF16)


def _row_blocks(rows, sub):
    sub = min(sub, rows)
    return [(r0, sub) for r0 in range(0, rows, sub)]


def _qk_kernel(n_side, h_ref, w_ref, cp_ref, sp_ref, cs_ref, ss_ref, src_ref, o_ref, dst_ref, wb):
    j = pl.program_id(0)
    i = pl.program_id(1)
    _side_cast(n_side, src_ref, dst_ref)

    @pl.when(i == 0)
    def _():
        wb[...] = w_ref[...].astype(BF16)

    scale = jnp.where(j < D_QK // TNW, 1.0, DK ** -0.5).astype(F32)
    half = DK // 2

    def run(rows, c_ref, s_ref):
        for r0, n in _row_blocks(rows, SUB_QK):
            z = jnp.dot(h_ref[r0:r0 + n, :], wb[...], preferred_element_type=F32)
            c = c_ref[r0:r0 + n, :]
            s = s_ref[r0:r0 + n, :]
            for hh in range(TNW // DK):
                x1 = z[:, hh * DK:hh * DK + half]
                x2 = z[:, hh * DK + half:(hh + 1) * DK]
                o_ref[r0:r0 + n, hh * DK:hh * DK + half] = ((x1 * c - x2 * s) * scale).astype(BF16)
                o_ref[r0:r0 + n, hh * DK + half:(hh + 1) * DK] = ((x2 * c + x1 * s) * scale).astype(BF16)

    @pl.when(i < NP)
    def _():
        run(TMP, cp_ref, sp_ref)

    @pl.when(i == NP)
    def _():
        run(ROWS_S, cs_ref, ss_ref)


def _qk_proj(h, w_in, rope, side):
    cos_p, sin_p, cos_s, sin_s = rope
    ncol = 2 * D_QK // TNW
    n_side = side.shape[0] // WCH
    src, dst = _side_specs(n_side)
    ptab = pl.BlockSpec((TMP, DK // 2), lambda j, i: (i % TPS, 0))
    stab = pl.BlockSpec((ROWS_S, DK // 2), lambda j, i: (0, 0))
    return pl.pallas_call(
        functools.partial(_qk_kernel, n_side),
        grid=(ncol, NT),
        in_specs=[_h_spec(), pl.BlockSpec((D_MODEL, TNW), lambda j, i: (0, _ahead(j, i, ncol))),
                  ptab, ptab, stab, stab, src],
        out_specs=[_o_spec(TNW), dst],
        out_shape=[jax.ShapeDtypeStruct((ROWS, 2 * D_QK), BF16), jax.ShapeDtypeStruct(side.shape, BF16)],
        scratch_shapes=[pltpu.VMEM((D_MODEL, TNW), BF16)],
        compiler_params=_params(("arbitrary", "arbitrary")),
        name="qk_proj",
    )(h, w_in, cos_p, sin_p, cos_s, sin_s, side)


def _plain_kernel(n_silu, n_side, h_ref, w_ref, src_ref, o_ref, dst_ref, wb):
    j = pl.program_id(0)
    i = pl.program_id(1)
    _side_cast(n_side, src_ref, dst_ref)

    @pl.when(i == 0)
    def _():
        wb[...] = w_ref[...].astype(BF16)

    def run(rows):
        for r0, n in _row_blocks(rows, SUB):
            z = jnp.dot(h_ref[r0:r0 + n, :], wb[...], preferred_element_type=F32)
            if n_silu is not None:
                z = jax.nn.sigmoid(z) * jnp.where(j < n_silu, z, 1.0)
            o_ref[r0:r0 + n, :] = z.astype(BF16)

    @pl.when(i < NP)
    def _():
        run(TMP)

    @pl.when(i == NP)
    def _():
        run(ROWS_S)


def _plain_proj(h, w_in, wcol, ncol, name, side, n_silu=None):
    n_side = side.shape[0] // WCH
    src, dst = _side_specs(n_side)
    return pl.pallas_call(
        functools.partial(_plain_kernel, n_silu, n_side),
        grid=(ncol, NT),
        in_specs=[_h_spec(), pl.BlockSpec((D_MODEL, TNW), lambda j, i: (0, wcol(_ahead(j, i, ncol)))), src],
        out_specs=[_o_spec(TNW), dst],
        out_shape=[jax.ShapeDtypeStruct((ROWS, ncol * TNW), BF16), jax.ShapeDtypeStruct(side.shape, BF16)],
        scratch_shapes=[pltpu.VMEM((D_MODEL, TNW), BF16)],
        compiler_params=_params(("arbitrary", "arbitrary")),
        name=name,
    )(h, w_in, side)


def _shift_rows(u, c0, c1):
    rows = lax.broadcasted_iota(jnp.int32, u.shape, 0)
    r1 = pltpu.roll(u, 1, 0)
    r2 = pltpu.roll(u, 2, 0)
    u1 = jnp.where(rows == 0, c1, r1)
    u2 = jnp.where(rows == 0, c0, jnp.where(rows == 1, c1, r2))
    return u1, u2


def _shift_slabs(u, s0, s1):
    n = DEC_BATCH
    u1 = jnp.concatenate([s1, u[0:(DEC_SEQ - 1) * n]], axis=0)
    u2 = jnp.concatenate([s0, s1, u[0:(DEC_SEQ - 2) * n]], axis=0)
    return u1, u2


def _state_in_specs(width):
    return [pl.BlockSpec((DEC_BATCH, TC), lambda j, i: (0, j)),
            pl.BlockSpec((DEC_BATCH, TC), lambda j, i: (0, width // TC + j))]


def _conv3(w, u, u1, u2):
    return w[0:1] * u2 + w[1:2] * u1 + w[2:3] * u


def _state_specs():
    return [pl.BlockSpec((BATCH, 2, TC), lambda j, i: (0, 0, j)),
            pl.BlockSpec((2 * DEC_BATCH, TC), lambda j, i: (0, j))]


def _state_shapes(width):
    return [jax.ShapeDtypeStruct((BATCH, 2, width), F32),
            jax.ShapeDtypeStruct((2 * DEC_BATCH, width), F32)]


def _conv_rows(i, carry, stp_ref, tile_fn):
    @pl.when(i % TPS == 0)
    def _():
        carry[...] = jnp.zeros_like(carry)

    for r0, n in _row_blocks(TMP, SUB):
        u = tile_fn(r0, n, carry[0:1, :], carry[1:2, :])
        carry[0:2, :] = u[n - 2:n, :]

    @pl.when(i % TPS == TPS - 1)
    def _():
        stp_ref[i // TPS] = carry[0:2, :]


def _convproj_kernel(h_ref, wgb_ref, wgc_ref, whc_ref, cw_ref, s0_ref, s1_ref, cb_ref, stp_ref, sts_ref, wb, carry):
    i = pl.program_id(1)

    @pl.when(i == 0)
    def _():
        wb[0] = wgb_ref[...].astype(BF16)
        wb[1] = wgc_ref[...].astype(BF16)
        wb[2] = whc_ref[...].astype(BF16)

    w = cw_ref[...]

    def proj(h):
        gb = jnp.dot(h, wb[0], preferred_element_type=F32)
        gc = jnp.dot(h, wb[1], preferred_element_type=F32)
        hc = jnp.dot(h, wb[2], preferred_element_type=F32)
        return gb, gc * hc

    def tile(r0, n, c0, c1):
        gb, u = proj(h_ref[r0:r0 + n, :])
        u1, u2 = _shift_rows(u, c0, c1)
        cb_ref[r0:r0 + n, :] = (gb * _conv3(w, u, u1, u2)).astype(BF16)
        return u

    @pl.when(i < NP)
    def _():
        _conv_rows(i, carry, stp_ref, tile)

    @pl.when(i == NP)
    def _():
        gb, u = proj(h_ref[0:ROWS_S, :])
        u1, u2 = _shift_slabs(u, s0_ref[...], s1_ref[...])
        cb_ref[0:ROWS_S, :] = (gb * _conv3(w, u, u1, u2)).astype(BF16)
        sts_ref[...] = u[(DEC_SEQ - 2) * DEC_BATCH:, :]


def _conv_proj(h, w_in, conv_w, st):
    ncol = D_CONV // TC

    def wspec(col0):
        return pl.BlockSpec((D_MODEL, TC), lambda j, i: (0, col0 // TC + _ahead(j, i, ncol)))
    return pl.pallas_call(
        _convproj_kernel,
        grid=(ncol, NT),
        in_specs=[_h_spec(), wspec(COL_GB), wspec(COL_GC), wspec(COL_HC),
                  pl.BlockSpec((3, TC), lambda j, i: (0, j))] + _state_in_specs(D_CONV),
        out_specs=[_o_spec(TC)] + _state_specs(),
        out_shape=[jax.ShapeDtypeStruct((ROWS, D_CONV), BF16)] + _state_shapes(D_CONV),
        scratch_shapes=[pltpu.VMEM((3, D_MODEL, TC), BF16), pltpu.VMEM((8, TC), F32)],
        compiler_params=_params(("arbitrary", "arbitrary")),
        name="conv_proj",
    )(h, w_in, w_in, w_in, conv_w, st, st)


def _tile_lanes(a, n):
    return jnp.concatenate([a] * n, axis=1)


def _gate_norm(o, sg):
    on = o * lax.rsqrt(jnp.mean(o * o, axis=-1, keepdims=True) + EPS)
    return (sg.astype(F32) * on).astype(BF16)


def _ret_p_kernel(gc_ref, q_ref, k_ref, v_ref, g_ref, dec_ref, ar_ref, cr_ref, og_ref, st_ref, s_scr):
    h = pl.program_id(1)
    gC = gc_ref[h]
    decay = dec_ref[0]
    arow = _tile_lanes(ar_ref[0], DV // 128)
    crow = _tile_lanes(cr_ref[0], DK // 128)
    s_scr[...] = jnp.zeros_like(s_scr)

    def chunk(c, carry):
        r = pl.multiple_of(c * CHUNK, CHUNK)
        q = q_ref[pl.ds(r, CHUNK), :]
        k = k_ref[pl.ds(r, CHUNK), :]
        v = v_ref[pl.ds(r, CHUNK), :]
        s_prev = s_scr[...]
        sc = lax.dot_general(q, k, (((1,), (1,)), ((), ())), preferred_element_type=F32) * decay
        inner = jnp.dot(sc.astype(BF16), v, preferred_element_type=F32)
        cross = jnp.dot(q, s_prev.astype(BF16), preferred_element_type=F32) * arow
        kdt = (k.astype(F32) * crow).T.astype(BF16)
        s_scr[...] = gC * s_prev + jnp.dot(kdt, v, preferred_element_type=F32)
        og_ref[pl.ds(r, CHUNK), :] = _gate_norm(inner + cross, g_ref[pl.ds(r, CHUNK), :])
        return carry

    lax.fori_loop(0, SEQ // CHUNK, chunk, 0, unroll=4)
    st_ref[0, 0] = s_scr[...]


def _ret_prompt(qk, v, gg, tables):
    decay, arow, crow, gC = tables
    tab = pl.BlockSpec((1, CHUNK, 128), lambda b, h: (h, 0, 0))
    return pl.pallas_call(
        _ret_p_kernel,
        grid=(BATCH, N_HEADS),
        in_specs=[
            pl.BlockSpec(memory_space=pltpu.SMEM),
            pl.BlockSpec((SEQ, DK), lambda b, h: (b, h)),
            pl.BlockSpec((SEQ, DK), lambda b, h: (b, N_HEADS + h)),
            pl.BlockSpec((SEQ, DV), lambda b, h: (b, h)),
            pl.BlockSpec((SEQ, DV), lambda b, h: (b, h)),
            pl.BlockSpec((1, CHUNK, CHUNK), lambda b, h: (h, 0, 0)), tab, tab,
        ],
        out_specs=[
            pl.BlockSpec((SEQ, DV), lambda b, h: (b, h)),
            pl.BlockSpec((1, 1, DK, DV), lambda b, h: (b, h, 0, 0)),
        ],
        out_shape=[
            jax.ShapeDtypeStruct((ROWS, D_V), BF16),
            jax.ShapeDtypeStruct((BATCH, N_HEADS, DK, DV), F32),
        ],
        scratch_shapes=[pltpu.VMEM((DK, DV), F32)],
        compiler_params=_params(("arbitrary", "arbitrary")),
        name="ret_prompt",
    )(gC, qk, qk, v, gg, decay, arow, crow)


def _ret_d_kernel(gc_ref, og_in_ref, q_ref, k_ref, v_ref, g_ref, s_ref, dec_ref, ar_ref, cr_ref,
                  og_ref, sn_ref, kdt_scr):
    del og_in_ref
    h = pl.program_id(0)
    g0 = pl.multiple_of(pl.program_id(1) * DEC_GROUP, DEC_GROUP)
    gC = gc_ref[h]

    def rows(t):
        return pl.ds(t * DEC_BATCH + g0, DEC_GROUP)

    def gather(ref):
        return jnp.concatenate([ref[rows(t), :] for t in range(DEC_SEQ)], axis=0)

    q = gather(q_ref)
    k = gather(k_ref)
    v = gather(v_ref)
    sc = lax.dot_general(q, k, (((1,), (1,)), ((), ())), preferred_element_type=F32) * dec_ref[0]
    inner = jnp.dot(sc.astype(BF16), v, preferred_element_type=F32)
    kd = k.astype(F32) * _tile_lanes(cr_ref[0], DK // 128)
    kdt_scr[...] = jnp.concatenate([kd, jnp.zeros_like(kd)], axis=0).T.astype(BF16)
    row_seq = lax.broadcasted_iota(jnp.int32, v.shape, 0) & (DEC_GROUP - 1)
    vpad = jnp.zeros_like(v)

    def body(bi, cross):
        s_prev = s_ref[0, bi, 0]
        mine = row_seq == bi
        res = jnp.dot(q, s_prev.astype(BF16), preferred_element_type=F32)
        vm = jnp.concatenate([jnp.where(mine, v, vpad), vpad], axis=0)
        sn_ref[0, bi, 0] = gC * s_prev + jnp.dot(kdt_scr[...], vm, preferred_element_type=F32)
        return cross + jnp.where(mine, res, 0.0)

    cross = lax.fori_loop(0, DEC_GROUP, body, jnp.zeros((DEC_LIVE, DV), F32))
    o = inner + cross * _tile_lanes(ar_ref[0], DV // 128)
    og = _gate_norm(o, gather(g_ref))
    for t in range(DEC_SEQ):
        og_ref[rows(t), :] = og[t * DEC_GROUP:(t + 1) * DEC_GROUP, :]


def _ret_decode(og, qk, v, gg, state, tables):
    decay, arow, crow, gC = tables
    tab = pl.BlockSpec((1, DEC_LIVE, 128), lambda h, g: (h, 0, 0))
    sspec = pl.BlockSpec((1, DEC_GROUP, 1, DK, DV), lambda h, g: (0, g, h, 0, 0))
    return pl.pallas_call(
        _ret_d_kernel,
        grid=(N_HEADS, DEC_BATCH // DEC_GROUP),
        in_specs=[
            pl.BlockSpec(memory_space=pltpu.SMEM),
            pl.BlockSpec(memory_space=pl.ANY),
            pl.BlockSpec((ROWS_S, DK), lambda h, g: (DEC_BLK, h)),
            pl.BlockSpec((ROWS_S, DK), lambda h, g: (DEC_BLK, N_HEADS + h)),
            pl.BlockSpec((ROWS_S, DV), lambda h, g: (DEC_BLK, h)),
            pl.BlockSpec((ROWS_S, DV), lambda h, g: (DEC_BLK, h)),
            sspec,
            pl.BlockSpec((1, DEC_LIVE, DEC_LIVE), lambda h, g: (h, 0, 0)), tab, tab,
        ],
        out_specs=[pl.BlockSpec((ROWS_S, DV), lambda h, g: (DEC_BLK, h)), sspec],
        out_shape=[
            jax.ShapeDtypeStruct((ROWS, D_V), BF16),
            jax.ShapeDtypeStruct(state.shape, F32),
        ],
        input_output_aliases={1: 0},
        scratch_shapes=[pltpu.VMEM((DK, 128), BF16)],
        compiler_params=_params(("arbitrary", "arbitrary")),
        name="ret_decode",
    )(gC, og, qk, qk, v, gg, state, decay, arow, crow)


def _resident(shape):
    return pl.BlockSpec(shape, lambda s: (0,) * len(shape), pipeline_mode=pl.Buffered(1))


def _tile_spec(width, col=0, tm=TMR):
    return pl.BlockSpec((tm, width), lambda s: (s, col))


def _merge_kernel(og_ref, cb_ref, ga_ref, gb_ref, pr_ref, pc_ref, m_ref):
    ya = jnp.dot(og_ref[...], pr_ref[...], preferred_element_type=F32)
    yb = jnp.dot(cb_ref[...], pc_ref[...], preferred_element_type=F32)
    m = ga_ref[...].astype(F32) * ya + gb_ref[...].astype(F32) * yb
    m_ref[...] = m.astype(BF16)


def _merge(og, cb, gg, p_ret, p_conv):
    ga_blk = D_V // D_MODEL
    return pl.pallas_call(
        _merge_kernel,
        grid=(NR,),
        in_specs=[_tile_spec(D_V), _tile_spec(D_CONV), _tile_spec(D_MODEL, ga_blk), _tile_spec(D_MODEL, ga_blk + 1),
                  _resident((D_V, D_MODEL)), _resident((D_CONV, D_MODEL))],
        out_specs=_tile_spec(D_MODEL),
        out_shape=jax.ShapeDtypeStruct((ROWS, D_MODEL), BF16),
        compiler_params=_params(("arbitrary",)),
        name="merge",
    )(og, cb, gg, gg, p_ret, p_conv)


TMO = 512
NO_P = ROWS_P // TMO


def _oproj_kernel(m_ref, xp_ref, xs_ref, w_ref, gpost_ref, gpre_ref, x1_ref, h2_ref):
    s = pl.program_id(0)

    def tile(x_of):
        for r0 in range(0, TMO, TMR):
            mix = jnp.dot(m_ref[r0:r0 + TMR, :], w_ref[...], preferred_element_type=F32)
            x1 = x_of(r0) + _rms(mix, gpost_ref[...])
            x1_ref[r0:r0 + TMR, :] = x1
            h2_ref[r0:r0 + TMR, :] = _rms(x1, gpre_ref[...]).astype(BF16)

    @pl.when(s < NO_P)
    def _():
        tile(lambda r0: xp_ref[r0:r0 + TMR, :])

    @pl.when(s >= NO_P)
    def _():
        tile(lambda r0: jnp.concatenate(
            [xs_ref[:, (r0 // DEC_BATCH + t) * D_MODEL:(r0 // DEC_BATCH + t + 1) * D_MODEL] for t in range(SLABS)],
            axis=0))


def _oproj(m, xp, xs, w_o, g_post, g_pre):
    row = _tile_spec(D_MODEL, tm=TMO)
    return pl.pallas_call(
        _oproj_kernel,
        grid=(ROWS // TMO,),
        in_specs=[row,
                  pl.BlockSpec((TMO, D_MODEL), lambda s: (jnp.minimum(s, NO_P - 1), 0)),
                  pl.BlockSpec((DEC_BATCH, DEC_SEQ * D_MODEL), lambda s: (0, 0)),
                  _resident((D_MODEL, D_MODEL)), _vec_spec(D_MODEL), _vec_spec(D_MODEL)],
        out_specs=[row, row],
        out_shape=[jax.ShapeDtypeStruct((ROWS, D_MODEL), F32), jax.ShapeDtypeStruct((ROWS, D_MODEL), BF16)],
        compiler_params=_params(("arbitrary",)),
        name="oproj",
    )(m, xp, xs, w_o, g_post, g_pre)


def _ffn1_kernel(n_side, h_ref, wu_ref, wg_ref, cw_ref, b_ref, s0_ref, s1_ref, src_ref,
                 a_ref, stp_ref, sts_ref, dst_ref, wb, carry):
    i = pl.program_id(1)
    _side_cast(n_side, src_ref, dst_ref)

    @pl.when(i == 0)
    def _():
        wb[0] = wu_ref[...].astype(BF16)
        wb[1] = wg_ref[...].astype(BF16)

    w = cw_ref[...]
    b = b_ref[...]

    def proj(h):
        up = jnp.dot(h, wb[0], preferred_element_type=F32)
        gate = jnp.dot(h, wb[1], preferred_element_type=F32)
        return up, gate

    def act(up, u1, u2, gate):
        return (jax.nn.gelu(_conv3(w, up, u1, u2) + b, approximate=True) * gate).astype(BF16)

    def tile(r0, n, c0, c1):
        up, gate = proj(h_ref[r0:r0 + n, :])
        u1, u2 = _shift_rows(up, c0, c1)
        a_ref[r0:r0 + n, :] = act(up, u1, u2, gate)
        return up

    @pl.when(i < NP)
    def _():
        _conv_rows(i, carry, stp_ref, tile)

    @pl.when(i == NP)
    def _():
        up, gate = proj(h_ref[0:ROWS_S, :])
        u1, u2 = _shift_slabs(up, s0_ref[...], s1_ref[...])
        a_ref[0:ROWS_S, :] = act(up, u1, u2, gate)
        sts_ref[...] = up[(DEC_SEQ - 2) * DEC_BATCH:, :]


def _ffn1(h, w_up, w_gate, conv_w, conv_b, st, side):
    ncol = D_FF // TC
    n_side = side.shape[0] // WCH
    src, dst = _side_specs(n_side)
    wspec = pl.BlockSpec((D_MODEL, TC), lambda j, i: (0, _ahead(j, i, ncol)))
    return pl.pallas_call(
        functools.partial(_ffn1_kernel, n_side),
        grid=(ncol, NT),
        in_specs=[_h_spec(), wspec, wspec,
                  pl.BlockSpec((3, TC), lambda j, i: (0, j)),
                  pl.BlockSpec((1, TC), lambda j, i: (0, j))] + _state_in_specs(D_FF) + [src],
        out_specs=[_o_spec(TC)] + _state_specs() + [dst],
        out_shape=([jax.ShapeDtypeStruct((ROWS, D_FF), BF16)] + _state_shapes(D_FF)
                   + [jax.ShapeDtypeStruct(side.shape, BF16)]),
        scratch_shapes=[pltpu.VMEM((2, D_MODEL, TC), BF16), pltpu.VMEM((8, TC), F32)],
        compiler_params=_params(("arbitrary", "arbitrary")),
        name="ffn_up",
    )(h, w_up, w_gate, conv_w, conv_b, st, st, side)


def _ffn2_kernel(a_ref, x_ref, w_ref, g_ref, yp_ref, ys_ref):
    s = pl.program_id(0)
    f = jnp.dot(a_ref[...], w_ref[...], preferred_element_type=F32)
    y = x_ref[...] + _rms(f, g_ref[...])

    @pl.when(s < NR_P)
    def _():
        yp_ref[...] = y

    @pl.when(s >= NR_P)
    def _():
        for t in range(SLABS):
            ys_ref[:, t * D_MODEL:(t + 1) * D_MODEL] = y[t * DEC_BATCH:(t + 1) * DEC_BATCH, :]


def _ffn2(a, x, w_down, g_post):
    return pl.pallas_call(
        _ffn2_kernel,
        grid=(NR,),
        in_specs=[_tile_spec(D_FF), _tile_spec(D_MODEL), _resident((D_FF, D_MODEL)), _vec_spec(D_MODEL)],
        out_specs=[pl.BlockSpec((TMR, D_MODEL), lambda s: (jnp.minimum(s, NR_P - 1), 0)),
                   pl.BlockSpec((DEC_BATCH, SLABS * D_MODEL), lambda s: (0, jnp.maximum(s - NR_P, 0)))],
        out_shape=[jax.ShapeDtypeStruct((ROWS_P, D_MODEL), F32),
                   jax.ShapeDtypeStruct((DEC_BATCH, DEC_SEQ * D_MODEL), F32)],
        compiler_params=_params(("arbitrary",)),
        name="ffn_down",
    )(a, x, w_down, g_post)


def _steps_on_lanes(a):
    return a.reshape(a.shape[0], a.shape[1] * a.shape[2])


def _from_time_major(a, t):
    return a.reshape(t, DEC_BATCH, a.shape[1]).transpose(1, 0, 2)


def _layer(xp, xs, s_ret, s_conv, s_ffn, rope, tabs_p, tabs_d,
           g_pre_mix, w_in, conv_w, p_ret, p_conv, w_o, g_post_mix,
           g_pre_ffn, w_up, w_gate, ffn_conv_w, ffn_conv_b, w_down, g_post_ffn):
    g_pre_mix = g_pre_mix.reshape(1, D_MODEL)
    g_post_mix = g_post_mix.reshape(1, D_MODEL)
    g_pre_ffn = g_pre_ffn.reshape(1, D_MODEL)
    g_post_ffn = g_post_ffn.reshape(1, D_MODEL)
    ffn_conv_b = ffn_conv_b.reshape(1, D_FF)

    h = _norm(xp, xs, g_pre_mix)
    qk, w_o = _qk_proj(h, w_in, rope, w_o)
    v, p_conv = _plain_proj(h, w_in, lambda j: COL_V // TNW + j, D_V // TNW, "v_proj", p_conv)
    n_g = D_V // TNW
    gg, p_ret = _plain_proj(h, w_in, lambda j: jnp.where(j < n_g, COL_G // TNW + j, COL_GA // TNW - n_g + j),
                            GG_W // TNW, "gate_proj", p_ret, n_silu=n_g)
    cb, conv_st_p, conv_st_s = _conv_proj(h, w_in, conv_w, _steps_on_lanes(s_conv))

    og, ret_st_p = _ret_prompt(qk, v, gg, tabs_p)
    og, ret_st_s = _ret_decode(og, qk, v, gg, s_ret[None], tabs_d)

    m = _merge(og, cb, gg, p_ret, p_conv)
    x1, h2 = _oproj(m, xp, xs, w_o, g_post_mix, g_pre_ffn)
    act, ffn_st_p, ffn_st_s, w_down = _ffn1(h2, w_up, w_gate, ffn_conv_w, ffn_conv_b, _steps_on_lanes(s_ffn),
                                            w_down)
    y_p, y_s = _ffn2(act, x1, w_down, g_post_ffn)
    return (y_p, y_s, ret_st_p, conv_st_p, ffn_st_p,
            ret_st_s[0], _from_time_major(conv_st_s, 2), _from_time_major(ffn_st_s, 2))


def kernel(x_prompt, x_sample, state_ret, state_conv, state_ffn, g_pre_mix, w_in, conv_w, p_ret, p_conv, w_o,
           g_post_mix, g_pre_ffn, w_up, w_gate, ffn_conv_w, ffn_conv_b, w_down, g_post_ffn):
    rope = _rope_tables()
    tabs_p = _ret_tables_prompt()
    tabs_d = _ret_tables_decode()
    depth = w_in.shape[0]
    xp = x_prompt.reshape(ROWS_P, D_MODEL)
    xs = _steps_on_lanes(x_sample)
    rp, cp, fp, rs, cs, fs = [], [], [], [], [], []
    for l in range(depth):
        xp, xs, a, b, c, d, e, f = _layer(
            xp, xs, state_ret[l], state_conv[l], state_ffn[l], rope, tabs_p, tabs_d,
            g_pre_mix[l], w_in[l], conv_w[l], p_ret[l], p_conv[l], w_o[l], g_post_mix[l],
            g_pre_ffn[l], w_up[l], w_gate[l], ffn_conv_w[l], ffn_conv_b[l], w_down[l], g_post_ffn[l])
        rp.append(a); cp.append(b); fp.append(c)
        rs.append(d); cs.append(e); fs.append(f)
    y_prompt = xp.reshape(BATCH, SEQ, D_MODEL)
    y_sample = xs.reshape(DEC_BATCH, DEC_SEQ, D_MODEL)
    return (y_prompt, y_sample, jnp.stack(rp), jnp.stack(cp), jnp.stack(fp),
            jnp.stack(rs), jnp.stack(cs), jnp.stack(fs))
```

```python
import functools

import numpy as np
import jax
import jax.numpy as jnp
from jax import lax
from jax.experimental import pallas as pl
from jax.experimental.pallas import tpu as pltpu

D_MODEL = 2048
BATCH = 4
SEQ = 2048
DEC_BATCH = 128
DEC_SEQ = 4
PAST_LEN = 16384
N_HEADS = 8
DK = D_MODEL // N_HEADS
DV = 2 * DK
D_QK = N_HEADS * DK
D_V = N_HEADS * DV
CHUNK = 256
ROPE_BASE = 10000.0
D_CONV = D_MODEL
D_FF = 5632
EPS = 1e-6
D_IN = 2 * D_QK + 2 * D_V + 3 * D_CONV + 2 * D_MODEL

F32 = jnp.float32
BF16 = jnp.bfloat16

ROWS_P = BATCH * SEQ
ROWS_S = DEC_SEQ * DEC_BATCH
ROWS = ROWS_P + ROWS_S
TMP = 1024
NP = ROWS_P // TMP
NT = NP + 1
TPS = SEQ // TMP
SUB_QK = 1024
SUB = 512
TNW = 1024
TC = 512
TMR = 256
SLABS = TMR // DEC_BATCH
NR_P = ROWS_P // TMR
NR = ROWS // TMR
WCH = 512
COL_V = 2 * D_QK
COL_G = COL_V + D_V
COL_GB = COL_G + D_V
COL_GC = COL_GB + D_CONV
COL_HC = COL_GC + D_CONV
COL_GA = COL_HC + D_CONV
GG_W = D_V + 2 * D_MODEL
DEC_GROUP = 16
DEC_LIVE = DEC_SEQ * DEC_GROUP
DEC_BLK = ROWS_P // ROWS_S
VMEM_LIMIT = 56 * 1024 * 1024


def _params(sem):
    return pltpu.CompilerParams(dimension_semantics=sem, vmem_limit_bytes=VMEM_LIMIT)


def _rms(x, g):
    ms = jnp.mean(x * x, axis=-1, keepdims=True)
    return x * lax.rsqrt(ms + EPS) * g


def _vec_spec(width):
    return pl.BlockSpec((1, width), lambda *_: (0, 0))


def _log_g():
    return np.log1p(-np.exp2(-5.0 - np.arange(N_HEADS, dtype=np.float64)))


def _rope_tables():
    half = DK // 2
    inv = ROPE_BASE ** (-np.arange(half, dtype=np.float64) / half)
    pos_p = np.arange(SEQ, dtype=np.float64)
    pos_s = np.repeat(PAST_LEN + np.arange(DEC_SEQ, dtype=np.float64), DEC_BATCH)
    out = []
    for pos in (pos_p, pos_s):
        ang = pos[:, None] * inv[None, :]
        out += [jnp.asarray(np.cos(ang), F32), jnp.asarray(np.sin(ang), F32)]
    return out


def _ret_tables_prompt():
    lg = _log_g()[:, None, None]
    i = np.arange(CHUNK, dtype=np.float64)
    diff = i[:, None] - i[None, :]
    decay = np.where(diff >= 0, np.exp(lg * np.where(diff >= 0, diff, 0.0)), 0.0)
    ones = np.ones((1, 1, 128))
    arow = np.exp(lg * (i + 1.0)[None, :, None]) * ones
    crow = np.exp(lg * (CHUNK - 1.0 - i)[None, :, None]) * ones
    gC = np.exp(_log_g() * CHUNK)
    return (jnp.asarray(decay, F32), jnp.asarray(arow, F32), jnp.asarray(crow, F32), jnp.asarray(gC, F32))


def _ret_tables_decode():
    lg = _log_g()[:, None, None]
    r = np.arange(128)
    seq, t = r % DEC_GROUP, (r // DEC_GROUP).astype(np.float64)
    valid = r < DEC_LIVE
    diff = t[:, None] - t[None, :]
    ok = (seq[:, None] == seq[None, :]) & (diff >= 0) & valid[:, None] & valid[None, :]
    decay = np.where(ok[None], np.exp(lg * np.where(ok, diff, 0.0)[None]), 0.0)[:, :DEC_LIVE, :DEC_LIVE]
    ones = np.ones((1, 1, 128))
    arow = (np.exp(lg * (t + 1.0)[None, :, None]) * ones)[:, :DEC_LIVE]
    crow = (np.exp(lg * (DEC_SEQ - 1.0 - t)[None, :, None]) * ones)[:, :DEC_LIVE]
    gC = np.exp(_log_g() * DEC_SEQ)
    return (jnp.asarray(decay, F32), jnp.asarray(arow, F32), jnp.asarray(crow, F32), jnp.asarray(gC, F32))


def _norm_kernel(xp_ref, xs_ref, g_ref, h_ref):
    i = pl.program_id(0)

    @pl.when(i < NP)
    def _():
        h_ref[...] = _rms(xp_ref[...], g_ref[...]).astype(BF16)

    @pl.when(i == NP)
    def _():
        for t in range(DEC_SEQ):
            x = xs_ref[:, t, :]
            h_ref[t * DEC_BATCH:(t + 1) * DEC_BATCH, :] = _rms(x, g_ref[...]).astype(BF16)


def _norm(xp, xs, g):
    return pl.pallas_call(
        _norm_kernel,
        grid=(NT,),
        in_specs=[pl.BlockSpec((TMP, D_MODEL), lambda i: (jnp.minimum(i, NP - 1), 0)),
                  pl.BlockSpec((DEC_BATCH, DEC_SEQ, D_MODEL), lambda i: (0, 0, 0)),
                  _vec_spec(D_MODEL)],
        out_specs=pl.BlockSpec((TMP, D_MODEL), lambda i: (i, 0)),
        out_shape=jax.ShapeDtypeStruct((ROWS, D_MODEL), BF16),
        compiler_params=_params(("arbitrary",)),
        name="norm",
    )(xp, xs, g)


def _h_spec():
    return pl.BlockSpec((TMP, D_MODEL), lambda j, i: (i, 0))


def _o_spec(tn):
    return pl.BlockSpec((TMP, tn), lambda j, i: (i, j))


def _ahead(j, i, ncol):
    return jnp.minimum(j + jnp.minimum(i, 1), ncol - 1)


def _side_specs(n_chunks):
    def idx(j, i):
        return (jnp.clip(j * NT + i - 1, 0, n_chunks - 1), 0)
    return pl.BlockSpec((WCH, D_MODEL), idx), pl.BlockSpec((WCH, D_MODEL), idx)


def _side_cast(n_chunks, src_ref, dst_ref):
    t = pl.program_id(0) * NT + pl.program_id(1)

    @pl.when((t >= 1) & (t <= n_chunks))
    def _():
        dst_ref[...] = src_ref[...].astype(BF16)


def _row_blocks(rows, sub):
    sub = min(sub, rows)
    return [(r0, sub) for r0 in range(0, rows, sub)]


def _qk_kernel(n_side, h_ref, w_ref, cp_ref, sp_ref, cs_ref, ss_ref, src_ref, o_ref, dst_ref, wb):
    j = pl.program_id(0)
    i = pl.program_id(1)
    _side_cast(n_side, src_ref, dst_ref)

    @pl.when(i == 0)
    def _():
        wb[...] = w_ref[...].astype(BF16)

    scale = jnp.where(j < D_QK // TNW, 1.0, DK ** -0.5).astype(F32)
    half = DK // 2

    def run(rows, c_ref, s_ref):
        for r0, n in _row_blocks(rows, SUB_QK):
            z = jnp.dot(h_ref[r0:r0 + n, :], wb[...], preferred_element_type=F32)
            c = c_ref[r0:r0 + n, :]
            s = s_ref[r0:r0 + n, :]
            for hh in range(TNW // DK):
                x1 = z[:, hh * DK:hh * DK + half]
                x2 = z[:, hh * DK + half:(hh + 1) * DK]
                o_ref[r0:r0 + n, hh * DK:hh * DK + half] = ((x1 * c - x2 * s) * scale).astype(BF16)
                o_ref[r0:r0 + n, hh * DK + half:(hh + 1) * DK] = ((x2 * c + x1 * s) * scale).astype(BF16)

    @pl.when(i < NP)
    def _():
        run(TMP, cp_ref, sp_ref)

    @pl.when(i == NP)
    def _():
        run(ROWS_S, cs_ref, ss_ref)


def _qk_proj(h, w_in, rope, side):
    cos_p, sin_p, cos_s, sin_s = rope
    ncol = 2 * D_QK // TNW
    n_side = side.shape[0] // WCH
    src, dst = _side_specs(n_side)
    ptab = pl.BlockSpec((TMP, DK // 2), lambda j, i: (i % TPS, 0))
    stab = pl.BlockSpec((ROWS_S, DK // 2), lambda j, i: (0, 0))
    return pl.pallas_call(
        functools.partial(_qk_kernel, n_side),
        grid=(ncol, NT),
        in_specs=[_h_spec(), pl.BlockSpec((D_MODEL, TNW), lambda j, i: (0, _ahead(j, i, ncol))),
                  ptab, ptab, stab, stab, src],
        out_specs=[_o_spec(TNW), dst],
        out_shape=[jax.ShapeDtypeStruct((ROWS, 2 * D_QK), BF16), jax.ShapeDtypeStruct(side.shape, BF16)],
        scratch_shapes=[pltpu.VMEM((D_MODEL, TNW), BF16)],
        compiler_params=_params(("arbitrary", "arbitrary")),
        name="qk_proj",
    )(h, w_in, cos_p, sin_p, cos_s, sin_s, side)


def _plain_kernel(n_silu, n_side, h_ref, w_ref, src_ref, o_ref, dst_ref, wb):
    j = pl.program_id(0)
    i = pl.program_id(1)
    _side_cast(n_side, src_ref, dst_ref)

    @pl.when(i == 0)
    def _():
        wb[...] = w_ref[...].astype(BF16)

    def run(rows):
        for r0, n in _row_blocks(rows, SUB):
            z = jnp.dot(h_ref[r0:r0 + n, :], wb[...], preferred_element_type=F32)
            if n_silu is not None:
                z = jax.nn.sigmoid(z) * jnp.where(j < n_silu, z, 1.0)
            o_ref[r0:r0 + n, :] = z.astype(BF16)

    @pl.when(i < NP)
    def _():
        run(TMP)

    @pl.when(i == NP)
    def _():
        run(ROWS_S)


def _plain_proj(h, w_in, wcol, ncol, name, side, n_silu=None):
    n_side = side.shape[0] // WCH
    src, dst = _side_specs(n_side)
    return pl.pallas_call(
        functools.partial(_plain_kernel, n_silu, n_side),
        grid=(ncol, NT),
        in_specs=[_h_spec(), pl.BlockSpec((D_MODEL, TNW), lambda j, i: (0, wcol(_ahead(j, i, ncol)))), src],
        out_specs=[_o_spec(TNW), dst],
        out_shape=[jax.ShapeDtypeStruct((ROWS, ncol * TNW), BF16), jax.ShapeDtypeStruct(side.shape, BF16)],
        scratch_shapes=[pltpu.VMEM((D_MODEL, TNW), BF16)],
        compiler_params=_params(("arbitrary", "arbitrary")),
        name=name,
    )(h, w_in, side)


def _shift_rows(u, c0, c1):
    rows = lax.broadcasted_iota(jnp.int32, u.shape, 0)
    r1 = pltpu.roll(u, 1, 0)
    r2 = pltpu.roll(u, 2, 0)
    u1 = jnp.where(rows == 0, c1, r1)
    u2 = jnp.where(rows == 0, c0, jnp.where(rows == 1, c1, r2))
    return u1, u2


def _shift_slabs(u, s0, s1):
    n = DEC_BATCH
    u1 = jnp.concatenate([s1, u[0:(DEC_SEQ - 1) * n]], axis=0)
    u2 = jnp.concatenate([s0, s1, u[0:(DEC_SEQ - 2) * n]], axis=0)
    return u1, u2


def _state_in_spec():
    return pl.BlockSpec((DEC_BATCH, 2, TC), lambda j, i: (0, 0, j), pipeline_mode=pl.Buffered(1))


def _conv3(w, u, u1, u2):
    return w[0:1] * u2 + w[1:2] * u1 + w[2:3] * u


def _state_specs():
    return [pl.BlockSpec((BATCH, 2, TC), lambda j, i: (0, 0, j)),
            pl.BlockSpec((DEC_BATCH, 2, TC), lambda j, i: (0, 0, j))]


def _state_shapes(width):
    return [jax.ShapeDtypeStruct((BATCH, 2, width), F32),
            jax.ShapeDtypeStruct((DEC_BATCH, 2, width), F32)]


def _conv_rows(i, carry, stp_ref, tile_fn):
    @pl.when(i % TPS == 0)
    def _():
        carry[...] = jnp.zeros_like(carry)

    for r0, n in _row_blocks(TMP, SUB):
        u = tile_fn(r0, n, carry[0:1, :], carry[1:2, :])
        carry[0:2, :] = u[n - 2:n, :]

    @pl.when(i % TPS == TPS - 1)
    def _():
        stp_ref[i // TPS] = carry[0:2, :]


def _convproj_kernel(h_ref, wgb_ref, wgc_ref, whc_ref, cw_ref, st_ref, cb_ref, stp_ref, sts_ref, wb, carry):
    i = pl.program_id(1)

    @pl.when(i == 0)
    def _():
        wb[0] = wgb_ref[...].astype(BF16)
        wb[1] = wgc_ref[...].astype(BF16)
        wb[2] = whc_ref[...].astype(BF16)

    w = cw_ref[...]

    def proj(h):
        gb = jnp.dot(h, wb[0], preferred_element_type=F32)
        gc = jnp.dot(h, wb[1], preferred_element_type=F32)
        hc = jnp.dot(h, wb[2], preferred_element_type=F32)
        return gb, gc * hc

    def tile(r0, n, c0, c1):
        gb, u = proj(h_ref[r0:r0 + n, :])
        u1, u2 = _shift_rows(u, c0, c1)
        cb_ref[r0:r0 + n, :] = (gb * _conv3(w, u, u1, u2)).astype(BF16)
        return u

    @pl.when(i < NP)
    def _():
        _conv_rows(i, carry, stp_ref, tile)

    @pl.when(i == NP)
    def _():
        gb, u = proj(h_ref[0:ROWS_S, :])
        u1, u2 = _shift_slabs(u, st_ref[:, 0, :], st_ref[:, 1, :])
        cb_ref[0:ROWS_S, :] = (gb * _conv3(w, u, u1, u2)).astype(BF16)
        for t in range(2):
            sts_ref[:, t, :] = u[(DEC_SEQ - 2 + t) * DEC_BATCH:(DEC_SEQ - 1 + t) * DEC_BATCH, :]


def _conv_proj(h, w_in, conv_w, st):
    ncol = D_CONV // TC

    def wspec(col0):
        return pl.BlockSpec((D_MODEL, TC), lambda j, i: (0, col0 // TC + _ahead(j, i, ncol)))
    return pl.pallas_call(
        _convproj_kernel,
        grid=(ncol, NT),
        in_specs=[_h_spec(), wspec(COL_GB), wspec(COL_GC), wspec(COL_HC),
                  pl.BlockSpec((3, TC), lambda j, i: (0, j)), _state_in_spec()],
        out_specs=[_o_spec(TC)] + _state_specs(),
        out_shape=[jax.ShapeDtypeStruct((ROWS, D_CONV), BF16)] + _state_shapes(D_CONV),
        scratch_shapes=[pltpu.VMEM((3, D_MODEL, TC), BF16), pltpu.VMEM((8, TC), F32)],
        compiler_params=_params(("arbitrary", "arbitrary")),
        name="conv_proj",
    )(h, w_in, w_in, w_in, conv_w, st)


def _tile_lanes(a, n):
    return jnp.concatenate([a] * n, axis=1)


def _gate_norm(o, sg):
    on = o * lax.rsqrt(jnp.mean(o * o, axis=-1, keepdims=True) + EPS)
    return (sg.astype(F32) * on).astype(BF16)


def _ret_p_kernel(gc_ref, q_ref, k_ref, v_ref, g_ref, dec_ref, ar_ref, cr_ref, og_ref, st_ref, s_scr):
    h = pl.program_id(1)
    gC = gc_ref[h]
    decay = dec_ref[0]
    arow = _tile_lanes(ar_ref[0], DV // 128)
    crow = _tile_lanes(cr_ref[0], DK // 128)
    s_scr[...] = jnp.zeros_like(s_scr)

    def chunk(c, carry):
        r = pl.multiple_of(c * CHUNK, CHUNK)
        q = q_ref[pl.ds(r, CHUNK), :]
        k = k_ref[pl.ds(r, CHUNK), :]
        v = v_ref[pl.ds(r, CHUNK), :]
        s_prev = s_scr[...]
        sc = lax.dot_general(q, k, (((1,), (1,)), ((), ())), preferred_element_type=F32) * decay
        inner = jnp.dot(sc.astype(BF16), v, preferred_element_type=F32)
        cross = jnp.dot(q, s_prev.astype(BF16), preferred_element_type=F32) * arow
        kdt = (k.astype(F32) * crow).T.astype(BF16)
        s_scr[...] = gC * s_prev + jnp.dot(kdt, v, preferred_element_type=F32)
        og_ref[pl.ds(r, CHUNK), :] = _gate_norm(inner + cross, g_ref[pl.ds(r, CHUNK), :])
        return carry

    lax.fori_loop(0, SEQ // CHUNK, chunk, 0, unroll=4)
    st_ref[0, 0] = s_scr[...]


def _ret_prompt(qk, v, gg, tables):
    decay, arow, crow, gC = tables
    tab = pl.BlockSpec((1, CHUNK, 128), lambda b, h: (h, 0, 0))
    return pl.pallas_call(
        _ret_p_kernel,
        grid=(BATCH, N_HEADS),
        in_specs=[
            pl.BlockSpec(memory_space=pltpu.SMEM),
            pl.BlockSpec((SEQ, DK), lambda b, h: (b, h)),
            pl.BlockSpec((SEQ, DK), lambda b, h: (b, N_HEADS + h)),
            pl.BlockSpec((SEQ, DV), lambda b, h: (b, h)),
            pl.BlockSpec((SEQ, DV), lambda b, h: (b, h)),
            pl.BlockSpec((1, CHUNK, CHUNK), lambda b, h: (h, 0, 0)), tab, tab,
        ],
        out_specs=[
            pl.BlockSpec((SEQ, DV), lambda b, h: (b, h)),
            pl.BlockSpec((1, 1, DK, DV), lambda b, h: (b, h, 0, 0)),
        ],
        out_shape=[
            jax.ShapeDtypeStruct((ROWS_P, D_V), BF16),
            jax.ShapeDtypeStruct((BATCH, N_HEADS, DK, DV), F32),
        ],
        scratch_shapes=[pltpu.VMEM((DK, DV), F32)],
        compiler_params=_params(("arbitrary", "arbitrary")),
        name="ret_prompt",
    )(gC, qk, qk, v, gg, decay, arow, crow)


def _ret_d_kernel(gc_ref, q_ref, k_ref, v_ref, g_ref, s_ref, dec_ref, ar_ref, cr_ref,
                  og_ref, sn_ref, kdt_scr):
    h = pl.program_id(0)
    g0 = pl.multiple_of(pl.program_id(1) * DEC_GROUP, DEC_GROUP)
    gC = gc_ref[h]

    def rows(t):
        return pl.ds(t * DEC_BATCH + g0, DEC_GROUP)

    def gather(ref):
        return jnp.concatenate([ref[rows(t), :] for t in range(DEC_SEQ)], axis=0)

    q = gather(q_ref)
    k = gather(k_ref)
    v = gather(v_ref)
    sc = lax.dot_general(q, k, (((1,), (1,)), ((), ())), preferred_element_type=F32) * dec_ref[0]
    inner = jnp.dot(sc.astype(BF16), v, preferred_element_type=F32)
    kd = k.astype(F32) * _tile_lanes(cr_ref[0], DK // 128)
    kdt_scr[...] = jnp.concatenate([kd, jnp.zeros_like(kd)], axis=0).T.astype(BF16)
    row_seq = lax.broadcasted_iota(jnp.int32, v.shape, 0) & (DEC_GROUP - 1)
    vpad = jnp.zeros_like(v)

    def body(bi, cross):
        s_prev = s_ref[0, bi, 0]
        mine = row_seq == bi
        res = jnp.dot(q, s_prev.astype(BF16), preferred_element_type=F32)
        vm = jnp.concatenate([jnp.where(mine, v, vpad), vpad], axis=0)
        sn_ref[0, bi, 0] = gC * s_prev + jnp.dot(kdt_scr[...], vm, preferred_element_type=F32)
        return cross + jnp.where(mine, res, 0.0)

    cross = lax.fori_loop(0, DEC_GROUP, body, jnp.zeros((DEC_LIVE, DV), F32))
    o = inner + cross * _tile_lanes(ar_ref[0], DV // 128)
    og = _gate_norm(o, gather(g_ref))
    for t in range(DEC_SEQ):
        og_ref[rows(t), :] = og[t * DEC_GROUP:(t + 1) * DEC_GROUP, :]


def _ret_decode(qk, v, gg, state, tables):
    decay, arow, crow, gC = tables
    tab = pl.BlockSpec((1, DEC_LIVE, 128), lambda h, g: (h, 0, 0))
    sspec = pl.BlockSpec((1, DEC_GROUP, 1, DK, DV), lambda h, g: (0, g, h, 0, 0))
    return pl.pallas_call(
        _ret_d_kernel,
        grid=(N_HEADS, DEC_BATCH // DEC_GROUP),
        in_specs=[
            pl.BlockSpec(memory_space=pltpu.SMEM),
            pl.BlockSpec((ROWS_S, DK), lambda h, g: (DEC_BLK, h)),
            pl.BlockSpec((ROWS_S, DK), lambda h, g: (DEC_BLK, N_HEADS + h)),
            pl.BlockSpec((ROWS_S, DV), lambda h, g: (DEC_BLK, h)),
            pl.BlockSpec((ROWS_S, DV), lambda h, g: (DEC_BLK, h)),
            sspec,
            pl.BlockSpec((1, DEC_LIVE, DEC_LIVE), lambda h, g: (h, 0, 0)), tab, tab,
        ],
        out_specs=[pl.BlockSpec((ROWS_S, DV), lambda h, g: (0, h)), sspec],
        out_shape=[
            jax.ShapeDtypeStruct((ROWS_S, D_V), BF16),
            jax.ShapeDtypeStruct(state.shape, F32),
        ],
        scratch_shapes=[pltpu.VMEM((DK, 128), BF16)],
        compiler_params=_params(("arbitrary", "arbitrary")),
        name="ret_decode",
    )(gC, qk, qk, v, gg, state, decay, arow, crow)


def _resident(shape):
    return pl.BlockSpec(shape, lambda s: (0,) * len(shape), pipeline_mode=pl.Buffered(1))


def _tile_spec(width, col=0, tm=TMR):
    return pl.BlockSpec((tm, width), lambda s: (s, col))


def _merge_kernel(ogp_ref, ogs_ref, cb_ref, ga_ref, gb_ref, pr_ref, pc_ref, m_ref):
    og = jnp.where(pl.program_id(0) < NR_P, ogp_ref[...], ogs_ref[...])
    ya = jnp.dot(og, pr_ref[...], preferred_element_type=F32)
    yb = jnp.dot(cb_ref[...], pc_ref[...], preferred_element_type=F32)
    m = ga_ref[...].astype(F32) * ya + gb_ref[...].astype(F32) * yb
    m_ref[...] = m.astype(BF16)


def _merge(og_p, og_s, cb, gg, p_ret, p_conv):
    ga_blk = D_V // D_MODEL
    return pl.pallas_call(
        _merge_kernel,
        grid=(NR,),
        in_specs=[pl.BlockSpec((TMR, D_V), lambda s: (jnp.minimum(s, NR_P - 1), 0)),
                  pl.BlockSpec((TMR, D_V), lambda s: (jnp.maximum(s - NR_P, 0), 0)),
                  _tile_spec(D_CONV), _tile_spec(D_MODEL, ga_blk), _tile_spec(D_MODEL, ga_blk + 1),
                  _resident((D_V, D_MODEL)), _resident((D_CONV, D_MODEL))],
        out_specs=_tile_spec(D_MODEL),
        out_shape=jax.ShapeDtypeStruct((ROWS, D_MODEL), BF16),
        compiler_params=_params(("arbitrary",)),
        name="merge",
    )(og_p, og_s, cb, gg, gg, p_ret, p_conv)


TMO = 512
NO_P = ROWS_P // TMO


def _oproj_kernel(m_ref, xp_ref, xs_ref, w_ref, gpost_ref, gpre_ref, x1_ref, h2_ref):
    s = pl.program_id(0)

    def tile(x_of):
        for r0 in range(0, TMO, TMR):
            mix = jnp.dot(m_ref[r0:r0 + TMR, :], w_ref[...], preferred_element_type=F32)
            x1 = x_of(r0) + _rms(mix, gpost_ref[...])
            x1_ref[r0:r0 + TMR, :] = x1
            h2_ref[r0:r0 + TMR, :] = _rms(x1, gpre_ref[...]).astype(BF16)

    @pl.when(s < NO_P)
    def _():
        tile(lambda r0: xp_ref[r0:r0 + TMR, :])

    @pl.when(s >= NO_P)
    def _():
        tile(lambda r0: jnp.concatenate([xs_ref[:, r0 // DEC_BATCH + t, :] for t in range(SLABS)], axis=0))


def _oproj(m, xp, xs, w_o, g_post, g_pre):
    row = _tile_spec(D_MODEL, tm=TMO)
    return pl.pallas_call(
        _oproj_kernel,
        grid=(ROWS // TMO,),
        in_specs=[row,
                  pl.BlockSpec((TMO, D_MODEL), lambda s: (jnp.minimum(s, NO_P - 1), 0)),
                  _resident((DEC_BATCH, DEC_SEQ, D_MODEL)),
                  _resident((D_MODEL, D_MODEL)), _vec_spec(D_MODEL), _vec_spec(D_MODEL)],
        out_specs=[row, row],
        out_shape=[jax.ShapeDtypeStruct((ROWS, D_MODEL), F32), jax.ShapeDtypeStruct((ROWS, D_MODEL), BF16)],
        compiler_params=_params(("arbitrary",)),
        name="oproj",
    )(m, xp, xs, w_o, g_post, g_pre)


def _ffn1_kernel(n_side, h_ref, wu_ref, wg_ref, cw_ref, b_ref, st_ref, src_ref,
                 a_ref, stp_ref, sts_ref, dst_ref, wb, carry):
    i = pl.program_id(1)
    _side_cast(n_side, src_ref, dst_ref)

    @pl.when(i == 0)
    def _():
        wb[0] = wu_ref[...].astype(BF16)
        wb[1] = wg_ref[...].astype(BF16)

    w = cw_ref[...]
    b = b_ref[...]

    def proj(h):
        up = jnp.dot(h, wb[0], preferred_element_type=F32)
        gate = jnp.dot(h, wb[1], preferred_element_type=F32)
        return up, gate

    def act(up, u1, u2, gate):
        return (jax.nn.gelu(_conv3(w, up, u1, u2) + b, approximate=True) * gate).astype(BF16)

    def tile(r0, n, c0, c1):
        up, gate = proj(h_ref[r0:r0 + n, :])
        u1, u2 = _shift_rows(up, c0, c1)
        a_ref[r0:r0 + n, :] = act(up, u1, u2, gate)
        return up

    @pl.when(i < NP)
    def _():
        _conv_rows(i, carry, stp_ref, tile)

    @pl.when(i == NP)
    def _():
        up, gate = proj(h_ref[0:ROWS_S, :])
        u1, u2 = _shift_slabs(up, st_ref[:, 0, :], st_ref[:, 1, :])
        a_ref[0:ROWS_S, :] = act(up, u1, u2, gate)
        for t in range(2):
            sts_ref[:, t, :] = up[(DEC_SEQ - 2 + t) * DEC_BATCH:(DEC_SEQ - 1 + t) * DEC_BATCH, :]


def _ffn1(h, w_up, w_gate, conv_w, conv_b, st, side):
    ncol = D_FF // TC
    n_side = side.shape[0] // WCH
    src, dst = _side_specs(n_side)
    wspec = pl.BlockSpec((D_MODEL, TC), lambda j, i: (0, _ahead(j, i, ncol)))
    return pl.pallas_call(
        functools.partial(_ffn1_kernel, n_side),
        grid=(ncol, NT),
        in_specs=[_h_spec(), wspec, wspec,
                  pl.BlockSpec((3, TC), lambda j, i: (0, j)),
                  pl.BlockSpec((1, TC), lambda j, i: (0, j)), _state_in_spec(), src],
        out_specs=[_o_spec(TC)] + _state_specs() + [dst],
        out_shape=([jax.ShapeDtypeStruct((ROWS, D_FF), BF16)] + _state_shapes(D_FF)
                   + [jax.ShapeDtypeStruct(side.shape, BF16)]),
        scratch_shapes=[pltpu.VMEM((2, D_MODEL, TC), BF16), pltpu.VMEM((8, TC), F32)],
        compiler_params=_params(("arbitrary", "arbitrary")),
        name="ffn_up",
    )(h, w_up, w_gate, conv_w, conv_b, st, side)


def _ffn2_kernel(a_ref, x_ref, w_ref, g_ref, yp_ref, ys_ref):
    s = pl.program_id(0)
    f = jnp.dot(a_ref[...], w_ref[...], preferred_element_type=F32)
    y = x_ref[...] + _rms(f, g_ref[...])

    @pl.when(s < NR_P)
    def _():
        yp_ref[...] = y

    for d in range(ROWS_S // TMR):
        @pl.when(s == NR_P + d)
        def _():
            for t in range(SLABS):
                ys_ref[:, d * SLABS + t, :] = y[t * DEC_BATCH:(t + 1) * DEC_BATCH, :]


def _ffn2(a, x, w_down, g_post):
    return pl.pallas_call(
        _ffn2_kernel,
        grid=(NR,),
        in_specs=[_tile_spec(D_FF), _tile_spec(D_MODEL), _resident((D_FF, D_MODEL)), _vec_spec(D_MODEL)],
        out_specs=[pl.BlockSpec((TMR, D_MODEL), lambda s: (jnp.minimum(s, NR_P - 1), 0)),
                   pl.BlockSpec((DEC_BATCH, DEC_SEQ, D_MODEL), lambda s: (0, 0, 0))],
        out_shape=[jax.ShapeDtypeStruct((ROWS_P, D_MODEL), F32),
                   jax.ShapeDtypeStruct((DEC_BATCH, DEC_SEQ, D_MODEL), F32)],
        compiler_params=_params(("arbitrary",)),
        name="ffn_down",
    )(a, x, w_down, g_post)


def _layer(xp, xs, s_ret, s_conv, s_ffn, rope, tabs_p, tabs_d,
           g_pre_mix, w_in, conv_w, p_ret, p_conv, w_o, g_post_mix,
           g_pre_ffn, w_up, w_gate, ffn_conv_w, ffn_conv_b, w_down, g_post_ffn):
    g_pre_mix = g_pre_mix.reshape(1, D_MODEL)
    g_post_mix = g_post_mix.reshape(1, D_MODEL)
    g_pre_ffn = g_pre_ffn.reshape(1, D_MODEL)
    g_post_ffn = g_post_ffn.reshape(1, D_MODEL)
    ffn_conv_b = ffn_conv_b.reshape(1, D_FF)

    h = _norm(xp, xs, g_pre_mix)
    qk, w_o = _qk_proj(h, w_in, rope, w_o)
    v, p_conv = _plain_proj(h, w_in, lambda j: COL_V // TNW + j, D_V // TNW, "v_proj", p_conv)
    n_g = D_V // TNW
    gg, p_ret = _plain_proj(h, w_in, lambda j: jnp.where(j < n_g, COL_G // TNW + j, COL_GA // TNW - n_g + j),
                            GG_W // TNW, "gate_proj", p_ret, n_silu=n_g)
    cb, conv_st_p, conv_st_s = _conv_proj(h, w_in, conv_w, s_conv)

    og_p, ret_st_p = _ret_prompt(qk, v, gg, tabs_p)
    og_s, ret_st_s = _ret_decode(qk, v, gg, s_ret[None], tabs_d)

    m = _merge(og_p, og_s, cb, gg, p_ret, p_conv)
    x1, h2 = _oproj(m, xp, xs, w_o, g_post_mix, g_pre_ffn)
    act, ffn_st_p, ffn_st_s, w_down = _ffn1(h2, w_up, w_gate, ffn_conv_w, ffn_conv_b, s_ffn,
                                            w_down)
    y_p, y_s = _ffn2(act, x1, w_down, g_post_ffn)
    return (y_p, y_s, ret_st_p, conv_st_p, ffn_st_p,
            ret_st_s[0], conv_st_s, ffn_st_s)


def kernel(x_prompt, x_sample, state_ret, state_conv, state_ffn, g_pre_mix, w_in, conv_w, p_ret, p_conv, w_o,
           g_post_mix, g_pre_ffn, w_up, w_gate, ffn_conv_w, ffn_conv_b, w_down, g_post_ffn):
    rope = _rope_tables()
    tabs_p = _ret_tables_prompt()
    tabs_d = _ret_tables_decode()
    depth = w_in.shape[0]
    xp = x_prompt.reshape(ROWS_P, D_MODEL)
    xs = x_sample
    rp, cp, fp, rs, cs, fs = [], [], [], [], [], []
    for l in range(depth):
        xp, xs, a, b, c, d, e, f = _layer(
            xp, xs, state_ret[l], state_conv[l], state_ffn[l], rope, tabs_p, tabs_d,
            g_pre_mix[l], w_in[l], conv_w[l], p_ret[l], p_conv[l], w_o[l], g_post_mix[l],
            g_pre_ffn[l], w_up[l], w_gate[l], ffn_conv_w[l], ffn_conv_b[l], w_down[l], g_post_ffn[l])
        rp.append(a); cp.append(b); fp.append(c)
        rs.append(d); cs.append(e); fs.append(f)
    y_prompt = xp.reshape(BATCH, SEQ, D_MODEL)
    y_sample = xs
    return (y_prompt, y_sample, jnp.stack(rp), jnp.stack(cp), jnp.stack(fp),
            jnp.stack(rs), jnp.stack(cs), jnp.stack(fs))
```

```python
import functools

import numpy as np
import jax
import jax.numpy as jnp
from jax import lax
from jax.experimental import pallas as pl
from jax.experimental.pallas import tpu as pltpu

D_MODEL = 2048
BATCH = 4
SEQ = 2048
DEC_BATCH = 128
DEC_SEQ = 4
PAST_LEN = 16384
N_HEADS = 8
DK = D_MODEL // N_HEADS
DV = 2 * DK
D_QK = N_HEADS * DK
D_V = N_HEADS * DV
CHUNK = 256
ROPE_BASE = 10000.0
D_CONV = D_MODEL
D_FF = 5632
EPS = 1e-6
D_IN = 2 * D_QK + 2 * D_V + 3 * D_CONV + 2 * D_MODEL

F32 = jnp.float32
BF16 = jnp.bfloat16

ROWS_P = BATCH * SEQ
ROWS_S = DEC_SEQ * DEC_BATCH
ROWS = ROWS_P + ROWS_S
TMP = 1024
NP = ROWS_P // TMP
NT = NP + 1
TPS = SEQ // TMP
SUB_QK = 1024
SUB_V = 512
SUB_GATE = 128
SUB_CONV = 512
SUB_FFN = 256
TNW = 1024
TC = 512
TMR = 256
SLABS = TMR // DEC_BATCH
NR_P = ROWS_P // TMR
NR = ROWS // TMR
WCH = 512
COL_V = 2 * D_QK
COL_G = COL_V + D_V
COL_GB = COL_G + D_V
COL_GC = COL_GB + D_CONV
COL_HC = COL_GC + D_CONV
COL_GA = COL_HC + D_CONV
GG_W = D_V + 2 * D_MODEL
DEC_GROUP = 16
DEC_LIVE = DEC_SEQ * DEC_GROUP
DEC_BLK = ROWS_P // ROWS_S
VMEM_LIMIT = 56 * 1024 * 1024


def _params(sem):
    return pltpu.CompilerParams(dimension_semantics=sem, vmem_limit_bytes=VMEM_LIMIT)


def _rms(x, g):
    ms = jnp.mean(x * x, axis=-1, keepdims=True)
    return x * lax.rsqrt(ms + EPS) * g


def _vec_spec(width):
    return pl.BlockSpec((1, width), lambda *_: (0, 0))


def _log_g():
    return np.log1p(-np.exp2(-5.0 - np.arange(N_HEADS, dtype=np.float64)))


def _rope_tables():
    half = DK // 2
    inv = ROPE_BASE ** (-np.arange(half, dtype=np.float64) / half)
    pos_p = np.arange(SEQ, dtype=np.float64)
    pos_s = np.repeat(PAST_LEN + np.arange(DEC_SEQ, dtype=np.float64), DEC_BATCH)
    out = []
    for pos in (pos_p, pos_s):
        ang = pos[:, None] * inv[None, :]
        out += [jnp.asarray(np.cos(ang), F32), jnp.asarray(np.sin(ang), F32)]
    return out


def _ret_tables_prompt():
    lg = _log_g()[:, None, None]
    i = np.arange(CHUNK, dtype=np.float64)
    diff = i[:, None] - i[None, :]
    decay = np.where(diff >= 0, np.exp(lg * np.where(diff >= 0, diff, 0.0)), 0.0)
    ones = np.ones((1, 1, 128))
    arow = np.exp(lg * (i + 1.0)[None, :, None]) * ones
    crow = np.exp(lg * (CHUNK - 1.0 - i)[None, :, None]) * ones
    gC = np.exp(_log_g() * CHUNK)
    return (jnp.asarray(decay, F32), jnp.asarray(arow, F32), jnp.asarray(crow, F32), jnp.asarray(gC, F32))


def _ret_tables_decode():
    lg = _log_g()[:, None, None]
    r = np.arange(128)
    seq, t = r % DEC_GROUP, (r // DEC_GROUP).astype(np.float64)
    valid = r < DEC_LIVE
    diff = t[:, None] - t[None, :]
    ok = (seq[:, None] == seq[None, :]) & (diff >= 0) & valid[:, None] & valid[None, :]
    decay = np.where(ok[None], np.exp(lg * np.where(ok, diff, 0.0)[None]), 0.0)[:, :DEC_LIVE, :DEC_LIVE]
    ones = np.ones((1, 1, 128))
    arow = (np.exp(lg * (t + 1.0)[None, :, None]) * ones)[:, :DEC_LIVE]
    crow = (np.exp(lg * (DEC_SEQ - 1.0 - t)[None, :, None]) * ones)[:, :DEC_LIVE]
    gC = np.exp(_log_g() * DEC_SEQ)
    return (jnp.asarray(decay, F32), jnp.asarray(arow, F32), jnp.asarray(crow, F32), jnp.asarray(gC, F32))


def _norm_kernel(xp_ref, xs_ref, g_ref, h_ref):
    i = pl.program_id(0)

    @pl.when(i < NP)
    def _():
        h_ref[...] = _rms(xp_ref[...], g_ref[...]).astype(BF16)

    @pl.when(i == NP)
    def _():
        for t in range(DEC_SEQ):
            x = xs_ref[:, t, :]
            h_ref[t * DEC_BATCH:(t + 1) * DEC_BATCH, :] = _rms(x, g_ref[...]).astype(BF16)


def _norm(xp, xs, g):
    return pl.pallas_call(
        _norm_kernel,
        grid=(NT,),
        in_specs=[pl.BlockSpec((TMP, D_MODEL), lambda i: (jnp.minimum(i, NP - 1), 0)),
                  pl.BlockSpec((DEC_BATCH, DEC_SEQ, D_MODEL), lambda i: (0, 0, 0)),
                  _vec_spec(D_MODEL)],
        out_specs=pl.BlockSpec((TMP, D_MODEL), lambda i: (i, 0)),
        out_shape=jax.ShapeDtypeStruct((ROWS, D_MODEL), BF16),
        compiler_params=_params(("arbitrary",)),
        name="norm",
    )(xp, xs, g)


def _h_spec():
    return pl.BlockSpec((TMP, D_MODEL), lambda j, i: (i, 0))


def _o_spec(tn):
    return pl.BlockSpec((TMP, tn), lambda j, i: (i, j))


def _ahead(j, i, ncol):
    return jnp.minimum(j + jnp.minimum(i, 1), ncol - 1)


def _side_specs(n_chunks):
    def idx(j, i):
        return (jnp.clip(j * NT + i - 1, 0, n_chunks - 1), 0)
    return pl.BlockSpec((WCH, D_MODEL), idx), pl.BlockSpec((WCH, D_MODEL), idx)


def _side_cast(n_chunks, src_ref, dst_ref):
    t = pl.program_id(0) * NT + pl.program_id(1)

    @pl.when((t >= 1) & (t <= n_chunks))
    def _():
        dst_ref[...] = src_ref[...].astype(BF16)


def _row_blocks(rows, sub):
    sub = min(sub, rows)
    return [(r0, sub) for r0 in range(0, rows, sub)]


def _qk_kernel(n_side, h_ref, w_ref, cp_ref, sp_ref, cs_ref, ss_ref, src_ref, o_ref, dst_ref, wb):
    j = pl.program_id(0)
    i = pl.program_id(1)
    _side_cast(n_side, src_ref, dst_ref)

    @pl.when(i == 0)
    def _():
        wb[...] = w_ref[...].astype(BF16)

    scale = jnp.where(j < D_QK // TNW, 1.0, DK ** -0.5).astype(F32)
    half = DK // 2

    def run(rows, c_ref, s_ref):
        for r0, n in _row_blocks(rows, SUB_QK):
            z = jnp.dot(h_ref[r0:r0 + n, :], wb[...], preferred_element_type=F32)
            c = c_ref[r0:r0 + n, :]
            s = s_ref[r0:r0 + n, :]
            for hh in range(TNW // DK):
                x1 = z[:, hh * DK:hh * DK + half]
                x2 = z[:, hh * DK + half:(hh + 1) * DK]
                o_ref[r0:r0 + n, hh * DK:hh * DK + half] = ((x1 * c - x2 * s) * scale).astype(BF16)
                o_ref[r0:r0 + n, hh * DK + half:(hh + 1) * DK] = ((x2 * c + x1 * s) * scale).astype(BF16)

    @pl.when(i < NP)
    def _():
        run(TMP, cp_ref, sp_ref)

    @pl.when(i == NP)
    def _():
        run(ROWS_S, cs_ref, ss_ref)


def _qk_proj(h, w_in, rope, side):
    cos_p, sin_p, cos_s, sin_s = rope
    ncol = 2 * D_QK // TNW
    n_side = side.shape[0] // WCH
    src, dst = _side_specs(n_side)
    ptab = pl.BlockSpec((TMP, DK // 2), lambda j, i: (i % TPS, 0))
    stab = pl.BlockSpec((ROWS_S, DK // 2), lambda j, i: (0, 0))
    return pl.pallas_call(
        functools.partial(_qk_kernel, n_side),
        grid=(ncol, NT),
        in_specs=[_h_spec(), pl.BlockSpec((D_MODEL, TNW), lambda j, i: (0, _ahead(j, i, ncol))),
                  ptab, ptab, stab, stab, src],
        out_specs=[_o_spec(TNW), dst],
        out_shape=[jax.ShapeDtypeStruct((ROWS, 2 * D_QK), BF16), jax.ShapeDtypeStruct(side.shape, BF16)],
        scratch_shapes=[pltpu.VMEM((D_MODEL, TNW), BF16)],
        compiler_params=_params(("arbitrary", "arbitrary")),
        name="qk_proj",
    )(h, w_in, cos_p, sin_p, cos_s, sin_s, side)


def _plain_kernel(sub, n_silu, n_side, h_ref, w_ref, src_ref, o_ref, dst_ref, wb):
    j = pl.program_id(0)
    i = pl.program_id(1)
    _side_cast(n_side, src_ref, dst_ref)

    @pl.when(i == 0)
    def _():
        wb[...] = w_ref[...].astype(BF16)

    def run(rows, silu):
        for r0, n in _row_blocks(rows, sub):
            z = jnp.dot(h_ref[r0:r0 + n, :], wb[...], preferred_element_type=F32)
            if silu:
                z = jax.nn.silu(z)
            o_ref[r0:r0 + n, :] = z.astype(BF16)

    def tiles(cond, silu):
        @pl.when(cond & (i < NP))
        def _():
            run(TMP, silu)

        @pl.when(cond & (i == NP))
        def _():
            run(ROWS_S, silu)

    if n_silu is None:
        tiles(j >= 0, False)
    else:
        tiles(j < n_silu, True)
        tiles(j >= n_silu, False)


def _plain_proj(h, w_in, wcol, ncol, name, side, sub, n_silu=None):
    n_side = side.shape[0] // WCH
    src, dst = _side_specs(n_side)
    return pl.pallas_call(
        functools.partial(_plain_kernel, sub, n_silu, n_side),
        grid=(ncol, NT),
        in_specs=[_h_spec(), pl.BlockSpec((D_MODEL, TNW), lambda j, i: (0, wcol(_ahead(j, i, ncol)))), src],
        out_specs=[_o_spec(TNW), dst],
        out_shape=[jax.ShapeDtypeStruct((ROWS, ncol * TNW), BF16), jax.ShapeDtypeStruct(side.shape, BF16)],
        scratch_shapes=[pltpu.VMEM((D_MODEL, TNW), BF16)],
        compiler_params=_params(("arbitrary", "arbitrary")),
        name=name,
    )(h, w_in, side)


def _shift_rows(u, c0, c1):
    rows = lax.broadcasted_iota(jnp.int32, u.shape, 0)
    r1 = pltpu.roll(u, 1, 0)
    r2 = pltpu.roll(u, 2, 0)
    u1 = jnp.where(rows == 0, c1, r1)
    u2 = jnp.where(rows == 0, c0, jnp.where(rows == 1, c1, r2))
    return u1, u2


def _shift_slabs(u, s0, s1):
    n = DEC_BATCH
    u1 = jnp.concatenate([s1, u[0:(DEC_SEQ - 1) * n]], axis=0)
    u2 = jnp.concatenate([s0, s1, u[0:(DEC_SEQ - 2) * n]], axis=0)
    return u1, u2


def _state_in_spec():
    return pl.BlockSpec((DEC_BATCH, 2, TC), lambda j, i: (0, 0, j), pipeline_mode=pl.Buffered(1))


def _conv3(w, u, u1, u2):
    return w[0:1] * u2 + w[1:2] * u1 + w[2:3] * u


def _state_specs():
    return [pl.BlockSpec((BATCH, 2, TC), lambda j, i: (0, 0, j)),
            pl.BlockSpec((DEC_BATCH, 2, TC), lambda j, i: (0, 0, j))]


def _state_shapes(width):
    return [jax.ShapeDtypeStruct((BATCH, 2, width), F32),
            jax.ShapeDtypeStruct((DEC_BATCH, 2, width), F32)]


def _conv_rows(i, carry, stp_ref, tile_fn, sub):
    @pl.when(i % TPS == 0)
    def _():
        carry[...] = jnp.zeros_like(carry)

    for r0, n in _row_blocks(TMP, sub):
        u = tile_fn(r0, n, carry[0:1, :], carry[1:2, :])
        carry[0:2, :] = u[n - 2:n, :]

    @pl.when(i % TPS == TPS - 1)
    def _():
        stp_ref[i // TPS] = carry[0:2, :]


def _convproj_kernel(h_ref, wgb_ref, wgc_ref, whc_ref, cw_ref, st_ref, cb_ref, stp_ref, sts_ref, wb, carry):
    i = pl.program_id(1)

    @pl.when(i == 0)
    def _():
        wb[0] = wgb_ref[...].astype(BF16)
        wb[1] = wgc_ref[...].astype(BF16)
        wb[2] = whc_ref[...].astype(BF16)

    w = cw_ref[...]

    def proj(h):
        gb = jnp.dot(h, wb[0], preferred_element_type=F32)
        gc = jnp.dot(h, wb[1], preferred_element_type=F32)
        hc = jnp.dot(h, wb[2], preferred_element_type=F32)
        return gb, gc * hc

    def tile(r0, n, c0, c1):
        gb, u = proj(h_ref[r0:r0 + n, :])
        u1, u2 = _shift_rows(u, c0, c1)
        cb_ref[r0:r0 + n, :] = (gb * _conv3(w, u, u1, u2)).astype(BF16)
        return u

    @pl.when(i < NP)
    def _():
        _conv_rows(i, carry, stp_ref, tile, SUB_CONV)

    @pl.when(i == NP)
    def _():
        gb, u = proj(h_ref[0:ROWS_S, :])
        u1, u2 = _shift_slabs(u, st_ref[:, 0, :], st_ref[:, 1, :])
        cb_ref[0:ROWS_S, :] = (gb * _conv3(w, u, u1, u2)).astype(BF16)
        for t in range(2):
            sts_ref[:, t, :] = u[(DEC_SEQ - 2 + t) * DEC_BATCH:(DEC_SEQ - 1 + t) * DEC_BATCH, :]


def _conv_proj(h, w_in, conv_w, st):
    ncol = D_CONV // TC

    def wspec(col0):
        return pl.BlockSpec((D_MODEL, TC), lambda j, i: (0, col0 // TC + _ahead(j, i, ncol)))
    return pl.pallas_call(
        _convproj_kernel,
        grid=(ncol, NT),
        in_specs=[_h_spec(), wspec(COL_GB), wspec(COL_GC), wspec(COL_HC),
                  pl.BlockSpec((3, TC), lambda j, i: (0, j)), _state_in_spec()],
        out_specs=[_o_spec(TC)] + _state_specs(),
        out_shape=[jax.ShapeDtypeStruct((ROWS, D_CONV), BF16)] + _state_shapes(D_CONV),
        scratch_shapes=[pltpu.VMEM((3, D_MODEL, TC), BF16), pltpu.VMEM((8, TC), F32)],
        compiler_params=_params(("arbitrary", "arbitrary")),
        name="conv_proj",
    )(h, w_in, w_in, w_in, conv_w, st)


def _tile_lanes(a, n):
    return jnp.concatenate([a] * n, axis=1)


def _gate_norm(o, sg):
    on = o * lax.rsqrt(jnp.mean(o * o, axis=-1, keepdims=True) + EPS)
    return (sg.astype(F32) * on).astype(BF16)


def _ret_p_kernel(gc_ref, q_ref, k_ref, v_ref, g_ref, dec_ref, ar_ref, cr_ref, og_ref, st_ref, s_scr):
    h = pl.program_id(1)
    gC = gc_ref[h]
    decay = dec_ref[0]
    arow = _tile_lanes(ar_ref[0], DV // 128)
    crow = _tile_lanes(cr_ref[0], DK // 128)
    s_scr[...] = jnp.zeros_like(s_scr)

    def chunk(c, carry):
        r = pl.multiple_of(c * CHUNK, CHUNK)
        q = q_ref[pl.ds(r, CHUNK), :]
        k = k_ref[pl.ds(r, CHUNK), :]
        v = v_ref[pl.ds(r, CHUNK), :]
        s_prev = s_scr[...]
        sc = lax.dot_general(q, k, (((1,), (1,)), ((), ())), preferred_element_type=F32) * decay
        inner = jnp.dot(sc.astype(BF16), v, preferred_element_type=F32)
        cross = jnp.dot(q, s_prev.astype(BF16), preferred_element_type=F32) * arow
        kdt = (k.astype(F32) * crow).T.astype(BF16)
        s_scr[...] = gC * s_prev + jnp.dot(kdt, v, preferred_element_type=F32)
        og_ref[pl.ds(r, CHUNK), :] = _gate_norm(inner + cross, g_ref[pl.ds(r, CHUNK), :])
        return carry

    lax.fori_loop(0, SEQ // CHUNK, chunk, 0, unroll=8)
    st_ref[0, 0] = s_scr[...]


def _ret_prompt(qk, v, gg, tables):
    decay, arow, crow, gC = tables
    tab = pl.BlockSpec((1, CHUNK, 128), lambda b, h: (h, 0, 0))
    return pl.pallas_call(
        _ret_p_kernel,
        grid=(BATCH, N_HEADS),
        in_specs=[
            pl.BlockSpec(memory_space=pltpu.SMEM),
            pl.BlockSpec((SEQ, DK), lambda b, h: (b, h)),
            pl.BlockSpec((SEQ, DK), lambda b, h: (b, N_HEADS + h)),
            pl.BlockSpec((SEQ, DV), lambda b, h: (b, h)),
            pl.BlockSpec((SEQ, DV), lambda b, h: (b, h)),
            pl.BlockSpec((1, CHUNK, CHUNK), lambda b, h: (h, 0, 0)), tab, tab,
        ],
        out_specs=[
            pl.BlockSpec((SEQ, DV), lambda b, h: (b, h)),
            pl.BlockSpec((1, 1, DK, DV), lambda b, h: (b, h, 0, 0)),
        ],
        out_shape=[
            jax.ShapeDtypeStruct((ROWS_P, D_V), BF16),
            jax.ShapeDtypeStruct((BATCH, N_HEADS, DK, DV), F32),
        ],
        scratch_shapes=[pltpu.VMEM((DK, DV), F32)],
        compiler_params=_params(("arbitrary", "arbitrary")),
        name="ret_prompt",
    )(gC, qk, qk, v, gg, decay, arow, crow)


def _ret_d_kernel(gc_ref, q_ref, k_ref, v_ref, g_ref, s_ref, dec_ref, ar_ref, cr_ref,
                  og_ref, sn_ref, kdt_scr):
    h = pl.program_id(0)
    g0 = pl.multiple_of(pl.program_id(1) * DEC_GROUP, DEC_GROUP)
    gC = gc_ref[h]

    def rows(t):
        return pl.ds(t * DEC_BATCH + g0, DEC_GROUP)

    def gather(ref):
        return jnp.concatenate([ref[rows(t), :] for t in range(DEC_SEQ)], axis=0)

    q = gather(q_ref)
    k = gather(k_ref)
    v = gather(v_ref)
    sc = lax.dot_general(q, k, (((1,), (1,)), ((), ())), preferred_element_type=F32) * dec_ref[0]
    inner = jnp.dot(sc.astype(BF16), v, preferred_element_type=F32)
    kd = k.astype(F32) * _tile_lanes(cr_ref[0], DK // 128)
    kdt_scr[...] = jnp.concatenate([kd, jnp.zeros_like(kd)], axis=0).T.astype(BF16)
    row_seq = lax.broadcasted_iota(jnp.int32, v.shape, 0) & (DEC_GROUP - 1)
    vpad = jnp.zeros_like(v)

    def body(bi, cross):
        s_prev = s_ref[0, bi, 0]
        mine = row_seq == bi
        res = jnp.dot(q, s_prev.astype(BF16), preferred_element_type=F32)
        vm = jnp.concatenate([jnp.where(mine, v, vpad), vpad], axis=0)
        sn_ref[0, bi, 0] = gC * s_prev + jnp.dot(kdt_scr[...], vm, preferred_element_type=F32)
        return cross + jnp.where(mine, res, 0.0)

    cross = lax.fori_loop(0, DEC_GROUP, body, jnp.zeros((DEC_LIVE, DV), F32))
    o = inner + cross * _tile_lanes(ar_ref[0], DV // 128)
    og = _gate_norm(o, gather(g_ref))
    for t in range(DEC_SEQ):
        og_ref[rows(t), :] = og[t * DEC_GROUP:(t + 1) * DEC_GROUP, :]


def _ret_decode(qk, v, gg, state, tables):
    decay, arow, crow, gC = tables
    tab = pl.BlockSpec((1, DEC_LIVE, 128), lambda h, g: (h, 0, 0))
    sspec = pl.BlockSpec((1, DEC_GROUP, 1, DK, DV), lambda h, g: (0, g, h, 0, 0))
    return pl.pallas_call(
        _ret_d_kernel,
        grid=(N_HEADS, DEC_BATCH // DEC_GROUP),
        in_specs=[
            pl.BlockSpec(memory_space=pltpu.SMEM),
            pl.BlockSpec((ROWS_S, DK), lambda h, g: (DEC_BLK, h)),
            pl.BlockSpec((ROWS_S, DK), lambda h, g: (DEC_BLK, N_HEADS + h)),
            pl.BlockSpec((ROWS_S, DV), lambda h, g: (DEC_BLK, h)),
            pl.BlockSpec((ROWS_S, DV), lambda h, g: (DEC_BLK, h)),
            sspec,
            pl.BlockSpec((1, DEC_LIVE, DEC_LIVE), lambda h, g: (h, 0, 0)), tab, tab,
        ],
        out_specs=[pl.BlockSpec((ROWS_S, DV), lambda h, g: (0, h)), sspec],
        out_shape=[
            jax.ShapeDtypeStruct((ROWS_S, D_V), BF16),
            jax.ShapeDtypeStruct(state.shape, F32),
        ],
        scratch_shapes=[pltpu.VMEM((DK, 128), BF16)],
        compiler_params=_params(("arbitrary", "arbitrary")),
        name="ret_decode",
    )(gC, qk, qk, v, gg, state, decay, arow, crow)


def _resident(shape):
    return pl.BlockSpec(shape, lambda s: (0,) * len(shape), pipeline_mode=pl.Buffered(1))


def _tile_spec(width, col=0, tm=TMR):
    return pl.BlockSpec((tm, width), lambda s: (s, col))


def _merge_kernel(ogp_ref, ogs_ref, cb_ref, ga_ref, gb_ref, pr_ref, pc_ref, m_ref):
    og = jnp.where(pl.program_id(0) < NR_P, ogp_ref[...], ogs_ref[...])
    ya = jnp.dot(og, pr_ref[...], preferred_element_type=F32)
    yb = jnp.dot(cb_ref[...], pc_ref[...], preferred_element_type=F32)
    m = jax.nn.sigmoid(ga_ref[...].astype(F32)) * ya + jax.nn.sigmoid(gb_ref[...].astype(F32)) * yb
    m_ref[...] = m.astype(BF16)


def _merge(og_p, og_s, cb, gg, p_ret, p_conv):
    ga_blk = D_V // D_MODEL
    return pl.pallas_call(
        _merge_kernel,
        grid=(NR,),
        in_specs=[pl.BlockSpec((TMR, D_V), lambda s: (jnp.minimum(s, NR_P - 1), 0)),
                  pl.BlockSpec((TMR, D_V), lambda s: (jnp.maximum(s - NR_P, 0), 0)),
                  _tile_spec(D_CONV), _tile_spec(D_MODEL, ga_blk), _tile_spec(D_MODEL, ga_blk + 1),
                  _resident((D_V, D_MODEL)), _resident((D_CONV, D_MODEL))],
        out_specs=_tile_spec(D_MODEL),
        out_shape=jax.ShapeDtypeStruct((ROWS, D_MODEL), BF16),
        compiler_params=_params(("arbitrary",)),
        name="merge",
    )(og_p, og_s, cb, gg, gg, p_ret, p_conv)


TMO = 512
NO_P = ROWS_P // TMO


def _oproj_kernel(m_ref, xp_ref, xs_ref, w_ref, gpost_ref, gpre_ref, x1_ref, h2_ref):
    s = pl.program_id(0)

    def tile(x_of):
        for r0 in range(0, TMO, TMR):
            mix = jnp.dot(m_ref[r0:r0 + TMR, :], w_ref[...], preferred_element_type=F32)
            x1 = x_of(r0) + _rms(mix, gpost_ref[...])
            x1_ref[r0:r0 + TMR, :] = x1
            h2_ref[r0:r0 + TMR, :] = _rms(x1, gpre_ref[...]).astype(BF16)

    @pl.when(s < NO_P)
    def _():
        tile(lambda r0: xp_ref[r0:r0 + TMR, :])

    @pl.when(s >= NO_P)
    def _():
        tile(lambda r0: jnp.concatenate([xs_ref[:, r0 // DEC_BATCH + t, :] for t in range(SLABS)], axis=0))


def _oproj(m, xp, xs, w_o, g_post, g_pre):
    row = _tile_spec(D_MODEL, tm=TMO)
    return pl.pallas_call(
        _oproj_kernel,
        grid=(ROWS // TMO,),
        in_specs=[row,
                  pl.BlockSpec((TMO, D_MODEL), lambda s: (jnp.minimum(s, NO_P - 1), 0)),
                  _resident((DEC_BATCH, DEC_SEQ, D_MODEL)),
                  _resident((D_MODEL, D_MODEL)), _vec_spec(D_MODEL), _vec_spec(D_MODEL)],
        out_specs=[row, row],
        out_shape=[jax.ShapeDtypeStruct((ROWS, D_MODEL), F32), jax.ShapeDtypeStruct((ROWS, D_MODEL), BF16)],
        compiler_params=_params(("arbitrary",)),
        name="oproj",
    )(m, xp, xs, w_o, g_post, g_pre)


def _ffn1_kernel(n_side, h_ref, wu_ref, wg_ref, cw_ref, b_ref, st_ref, src_ref,
                 a_ref, stp_ref, sts_ref, dst_ref, wb, carry):
    i = pl.program_id(1)
    _side_cast(n_side, src_ref, dst_ref)

    @pl.when(i == 0)
    def _():
        wb[0] = wu_ref[...].astype(BF16)
        wb[1] = wg_ref[...].astype(BF16)

    w = cw_ref[...]
    b = b_ref[...]

    def proj(h):
        up = jnp.dot(h, wb[0], preferred_element_type=F32)
        gate = jnp.dot(h, wb[1], preferred_element_type=F32)
        return up, gate

    def act(up, u1, u2, gate):
        return (jax.nn.gelu(_conv3(w, up, u1, u2) + b, approximate=True) * gate).astype(BF16)

    def tile(r0, n, c0, c1):
        up, gate = proj(h_ref[r0:r0 + n, :])
        u1, u2 = _shift_rows(up, c0, c1)
        a_ref[r0:r0 + n, :] = act(up, u1, u2, gate)
        return up

    @pl.when(i < NP)
    def _():
        _conv_rows(i, carry, stp_ref, tile, SUB_FFN)

    @pl.when(i == NP)
    def _():
        up, gate = proj(h_ref[0:ROWS_S, :])
        u1, u2 = _shift_slabs(up, st_ref[:, 0, :], st_ref[:, 1, :])
        a_ref[0:ROWS_S, :] = act(up, u1, u2, gate)
        for t in range(2):
            sts_ref[:, t, :] = up[(DEC_SEQ - 2 + t) * DEC_BATCH:(DEC_SEQ - 1 + t) * DEC_BATCH, :]


def _ffn1(h, w_up, w_gate, conv_w, conv_b, st, side):
    ncol = D_FF // TC
    n_side = side.shape[0] // WCH
    src, dst = _side_specs(n_side)
    wspec = pl.BlockSpec((D_MODEL, TC), lambda j, i: (0, _ahead(j, i, ncol)))
    return pl.pallas_call(
        functools.partial(_ffn1_kernel, n_side),
        grid=(ncol, NT),
        in_specs=[_h_spec(), wspec, wspec,
                  pl.BlockSpec((3, TC), lambda j, i: (0, j)),
                  pl.BlockSpec((1, TC), lambda j, i: (0, j)), _state_in_spec(), src],
        out_specs=[_o_spec(TC)] + _state_specs() + [dst],
        out_shape=([jax.ShapeDtypeStruct((ROWS, D_FF), BF16)] + _state_shapes(D_FF)
                   + [jax.ShapeDtypeStruct(side.shape, BF16)]),
        scratch_shapes=[pltpu.VMEM((2, D_MODEL, TC), BF16), pltpu.VMEM((8, TC), F32)],
        compiler_params=_params(("arbitrary", "arbitrary")),
        name="ffn_up",
    )(h, w_up, w_gate, conv_w, conv_b, st, side)


def _ffn2_kernel(a_ref, x_ref, w_ref, g_ref, yp_ref, ys_ref):
    s = pl.program_id(0)
    f = jnp.dot(a_ref[...], w_ref[...], preferred_element_type=F32)
    y = x_ref[...] + _rms(f, g_ref[...])

    @pl.when(s < NR_P)
    def _():
        yp_ref[...] = y

    for d in range(ROWS_S // TMR):
        @pl.when(s == NR_P + d)
        def _():
            for t in range(SLABS):
                ys_ref[:, d * SLABS + t, :] = y[t * DEC_BATCH:(t + 1) * DEC_BATCH, :]


def _ffn2(a, x, w_down, g_post):
    return pl.pallas_call(
        _ffn2_kernel,
        grid=(NR,),
        in_specs=[_tile_spec(D_FF), _tile_spec(D_MODEL), _resident((D_FF, D_MODEL)), _vec_spec(D_MODEL)],
        out_specs=[pl.BlockSpec((TMR, D_MODEL), lambda s: (jnp.minimum(s, NR_P - 1), 0)),
                   pl.BlockSpec((DEC_BATCH, DEC_SEQ, D_MODEL), lambda s: (0, 0, 0))],
        out_shape=[jax.ShapeDtypeStruct((ROWS_P, D_MODEL), F32),
                   jax.ShapeDtypeStruct((DEC_BATCH, DEC_SEQ, D_MODEL), F32)],
        compiler_params=_params(("arbitrary",)),
        name="ffn_down",
    )(a, x, w_down, g_post)


def _layer(xp, xs, s_ret, s_conv, s_ffn, rope, tabs_p, tabs_d,
           g_pre_mix, w_in, conv_w, p_ret, p_conv, w_o, g_post_mix,
           g_pre_ffn, w_up, w_gate, ffn_conv_w, ffn_conv_b, w_down, g_post_ffn):
    g_pre_mix = g_pre_mix.reshape(1, D_MODEL)
    g_post_mix = g_post_mix.reshape(1, D_MODEL)
    g_pre_ffn = g_pre_ffn.reshape(1, D_MODEL)
    g_post_ffn = g_post_ffn.reshape(1, D_MODEL)
    ffn_conv_b = ffn_conv_b.reshape(1, D_FF)

    h = _norm(xp, xs, g_pre_mix)
    qk, w_o = _qk_proj(h, w_in, rope, w_o)
    v, p_conv = _plain_proj(h, w_in, lambda j: COL_V // TNW + j, D_V // TNW, "v_proj", p_conv, SUB_V)
    n_g = D_V // TNW
    gg, p_ret = _plain_proj(h, w_in, lambda j: jnp.where(j < n_g, COL_G // TNW + j, COL_GA // TNW - n_g + j),
                            GG_W // TNW, "gate_proj", p_ret, SUB_GATE, n_silu=n_g)
    cb, conv_st_p, conv_st_s = _conv_proj(h, w_in, conv_w, s_conv)

    og_p, ret_st_p = _ret_prompt(qk, v, gg, tabs_p)
    og_s, ret_st_s = _ret_decode(qk, v, gg, s_ret[None], tabs_d)

    m = _merge(og_p, og_s, cb, gg, p_ret, p_conv)
    x1, h2 = _oproj(m, xp, xs, w_o, g_post_mix, g_pre_ffn)
    act, ffn_st_p, ffn_st_s, w_down = _ffn1(h2, w_up, w_gate, ffn_conv_w, ffn_conv_b, s_ffn,
                                            w_down)
    y_p, y_s = _ffn2(act, x1, w_down, g_post_ffn)
    return (y_p, y_s, ret_st_p, conv_st_p, ffn_st_p,
            ret_st_s[0], conv_st_s, ffn_st_s)


def kernel(x_prompt, x_sample, state_ret, state_conv, state_ffn, g_pre_mix, w_in, conv_w, p_ret, p_conv, w_o,
           g_post_mix, g_pre_ffn, w_up, w_gate, ffn_conv_w, ffn_conv_b, w_down, g_post_ffn):
    rope = _rope_tables()
    tabs_p = _ret_tables_prompt()
    tabs_d = _ret_tables_decode()
    depth = w_in.shape[0]
    xp = x_prompt.reshape(ROWS_P, D_MODEL)
    xs = x_sample
    rp, cp, fp, rs, cs, fs = [], [], [], [], [], []
    for l in range(depth):
        xp, xs, a, b, c, d, e, f = _layer(
            xp, xs, state_ret[l], state_conv[l], state_ffn[l], rope, tabs_p, tabs_d,
            g_pre_mix[l], w_in[l], conv_w[l], p_ret[l], p_conv[l], w_o[l], g_post_mix[l],
            g_pre_ffn[l], w_up[l], w_gate[l], ffn_conv_w[l], ffn_conv_b[l], w_down[l], g_post_ffn[l])
        rp.append(a); cp.append(b); fp.append(c)
        rs.append(d); cs.append(e); fs.append(f)
    y_prompt = xp.reshape(BATCH, SEQ, D_MODEL)
    y_sample = xs
    return (y_prompt, y_sample, jnp.stack(rp), jnp.stack(cp), jnp.stack(fp),
            jnp.stack(rs), jnp.stack(cs), jnp.stack(fs))
```
